```python
import math
import jax, jax.numpy as jnp
from jax import lax
import numpy as np

D_MODEL = 1024
BATCH = 8
SEQ = 2048
DEPTH = 2
DEC_BATCH = 128
DEC_SEQ = 1
PAST_LEN = 16384
PAGE_SIZE = 128

CONV_W = 1024
CONV_K_A = 3
SSM_INNER = 2 * D_MODEL
SSM_HEADDIM = 64
SSM_HEADS = SSM_INNER // SSM_HEADDIM
SSM_GROUPS = 4
SSM_STATE = 128
CONV_K_S = 4
SSM_XBC = SSM_INNER + 2 * SSM_GROUPS * SSM_STATE
SSD_CHUNK = 128
N_MEM = 256
MEM_HEADS = 4
MEM_HEADDIM = 256
ATTN_W = MEM_HEADS * MEM_HEADDIM
N_BRANCH = 3
EPS = 1e-6
IN_SPLIT_WIDTHS = (CONV_W, CONV_W, CONV_W, CONV_W, SSM_INNER, SSM_XBC, SSM_HEADS, ATTN_W, ATTN_W, N_BRANCH * D_MODEL)
D_IN_TOTAL = 4 * CONV_W + SSM_INNER + SSM_XBC + SSM_HEADS + 2 * ATTN_W + N_BRANCH * D_MODEL

kernel_name = "hybrid_shortconv_ssd_memattn_step"


def rmsnorm(x, w):
    xf = x.astype(jnp.float32)
    y = xf * lax.rsqrt(jnp.mean(xf * xf, axis=-1, keepdims=True) + EPS)
    return (y * w.astype(jnp.float32)).astype(x.dtype)


def gated_rmsnorm(y, z, w):
    bt, L, dd = y.shape
    g = (y * jax.nn.silu(z)).astype(jnp.float32).reshape(bt, L, SSM_GROUPS, dd // SSM_GROUPS)
    g = g * lax.rsqrt(jnp.mean(g * g, axis=-1, keepdims=True) + EPS)
    return (g.reshape(bt, L, dd) * w.astype(jnp.float32)).astype(y.dtype)


def causal_conv(u, buf, w, bias=None):
    k, c = w.shape
    full = jnp.concatenate([buf.astype(u.dtype), u], axis=1)
    y = lax.conv_general_dilated(full, w[:, None, :].astype(u.dtype), window_strides=(1,), padding='VALID',
                                 dimension_numbers=('NWC', 'WIO', 'NWC'), feature_group_count=c)
    if bias is not None:
        y = y + bias.astype(u.dtype)
    return y, full[:, -(k - 1):]


def ssd_scan(x, dt, a, b, c, h0):
    bt, L, H, P = x.shape
    G, N = b.shape[2], b.shape[3]
    HG = H // G
    q = SSD_CHUNK if L >= SSD_CHUNK else L
    nc = -(-L // q)
    pad = nc * q - L
    f32 = jnp.float32
    xf, bf, cf, dtf = x.astype(f32), b.astype(f32), c.astype(f32), dt.astype(f32)
    if pad:
        xf = jnp.pad(xf, ((0, 0), (0, pad), (0, 0), (0, 0)))
        bf = jnp.pad(bf, ((0, 0), (0, pad), (0, 0), (0, 0)))
        cf = jnp.pad(cf, ((0, 0), (0, pad), (0, 0), (0, 0)))
        dtf = jnp.pad(dtf, ((0, 0), (0, pad), (0, 0)))
    xf = xf.reshape(bt, nc, q, G, HG, P)
    dtf = dtf.reshape(bt, nc, q, G, HG)
    bf = bf.reshape(bt, nc, q, G, N)
    cf = cf.reshape(bt, nc, q, G, N)
    da = dtf * a.astype(f32).reshape(G, HG)
    cs = jnp.cumsum(da, axis=2)
    xdt = xf * dtf[..., None]
    seg = cs[:, :, :, None] - cs[:, :, None, :]
    mask = jnp.tril(jnp.ones((q, q), dtype=bool))[:, :, None, None]
    lmat = jnp.exp(jnp.where(mask, seg, -jnp.inf))
    scores = jnp.einsum('bclgn,bcsgn->bclsg', cf, bf)
    y_diag = jnp.einsum('bclsgh,bcsghp->bclghp', scores[..., None] * lmat, xdt)
    decay = jnp.exp(cs[:, :, -1:] - cs)
    st = jnp.einsum('bclgn,bclghp->bcghpn', bf, xdt * decay[..., None])
    total = jnp.exp(cs[:, :, -1])

    def step(h, inp):
        tot, s = inp
        return tot[..., None, None] * h + s, h

    h_init = h0.astype(f32).reshape(bt, G, HG, P, N)
    h_fin, h_prev = lax.scan(step, h_init, (jnp.moveaxis(total, 1, 0), jnp.moveaxis(st, 1, 0)))
    h_prev = jnp.moveaxis(h_prev, 0, 1)
    y_off = jnp.einsum('bclgn,bcghpn->bclghp', cf, h_prev) * jnp.exp(cs)[..., None]
    y = (y_diag + y_off).reshape(bt, nc * q, H, P)[:, :L]
    return y.astype(x.dtype), h_fin.reshape(bt, H, P, N).astype(h0.dtype)


def mem_kv(mem, mem_norm_w, w_mem_kv):
    bt = mem.shape[0]
    kv = rmsnorm(mem, mem_norm_w) @ w_mem_kv
    k, v = jnp.split(kv, 2, axis=-1)
    return (k.reshape(bt, N_MEM, MEM_HEADS, MEM_HEADDIM), v.reshape(bt, N_MEM, MEM_HEADS, MEM_HEADDIM))


def mem_attention(q, k, v):
    s = jnp.einsum('blhd,bmhd->bhlm', q, k).astype(jnp.float32) * (1.0 / math.sqrt(MEM_HEADDIM))
    p = jax.nn.softmax(s, axis=-1).astype(v.dtype)
    return jnp.einsum('bhlm,bmhd->blhd', p, v)


def mixer_layer(x, mk, mv, buf_a, buf_s, h0, norm_w, w_in, conv_a_w, w_a_out, conv_s_w, conv_s_b,
                dt_bias, a_log, d_skip, ssm_norm_w, w_s_out, w_m_out, w_o):
    bt, L, _ = x.shape
    u = rmsnorm(x, norm_w)
    proj = u @ w_in
    idx = [int(i) for i in np.cumsum(IN_SPLIT_WIDTHS)[:-1]]
    cb, cc, ch, cz, sz, sxbc, sdt, mq, mz, gr = jnp.split(proj, idx, axis=-1)
    conv_out, new_buf_a = causal_conv(cc * ch, buf_a, conv_a_w)
    y_a = (jax.nn.silu(cz) * cb * conv_out) @ w_a_out
    xbc, new_buf_s = causal_conv(sxbc, buf_s, conv_s_w, conv_s_b)
    xbc = jax.nn.silu(xbc)
    xs, bm, cm = jnp.split(xbc, [SSM_INNER, SSM_INNER + SSM_GROUPS * SSM_STATE], axis=-1)
    xs = xs.reshape(bt, L, SSM_HEADS, SSM_HEADDIM)
    bm = bm.reshape(bt, L, SSM_GROUPS, SSM_STATE)
    cm = cm.reshape(bt, L, SSM_GROUPS, SSM_STATE)
    dt = jax.nn.softplus(sdt.astype(jnp.float32) + dt_bias.astype(jnp.float32))
    a = -jnp.exp(a_log.astype(jnp.float32))
    y, h_new = ssd_scan(xs, dt, a, bm, cm, h0)
    y = y + d_skip[:, None].astype(y.dtype) * xs
    y_s = gated_rmsnorm(y.reshape(bt, L, SSM_INNER), sz, ssm_norm_w) @ w_s_out
    o = mem_attention(mq.reshape(bt, L, MEM_HEADS, MEM_HEADDIM), mk, mv)
    y_m = (jax.nn.silu(mz) * o.reshape(bt, L, ATTN_W)) @ w_m_out
    g = jax.nn.sigmoid(gr.astype(jnp.float32)).astype(x.dtype).reshape(bt, L, N_BRANCH, D_MODEL)
    merged = g[:, :, 0] * y_a + g[:, :, 1] * y_s + g[:, :, 2] * y_m
    return x + merged @ w_o, new_buf_a, new_buf_s, h_new


def setup_inputs(seed: int = 0) -> dict:
    key = jax.random.key(seed)
    ks = jax.random.split(key, 24)

    def nrm(k, shape, scale):
        return jax.random.normal(k, shape, jnp.float32) * scale

    dt0 = jnp.exp(jax.random.uniform(ks[14], (DEPTH, SSM_HEADS)) * (math.log(0.1) - math.log(0.001)) + math.log(0.001))
    return {
        "x_prompt": nrm(ks[0], (BATCH, SEQ, D_MODEL), 1.0),
        "x_sample": nrm(ks[1], (DEC_BATCH, DEC_SEQ, D_MODEL), 1.0),
        "mem_prompt": nrm(ks[2], (BATCH, N_MEM, D_MODEL), 1.0),
        "state_conv_a": nrm(ks[3], (DEPTH, DEC_BATCH, CONV_K_A - 1, CONV_W), 1.0),
        "state_conv_s": nrm(ks[4], (DEPTH, DEC_BATCH, CONV_K_S - 1, SSM_XBC), 1.0),
        "state_ssm": nrm(ks[5], (DEPTH, DEC_BATCH, SSM_HEADS, SSM_HEADDIM, SSM_STATE), 0.1),
        "cache_mem_k": nrm(ks[6], (DEPTH, DEC_BATCH, N_MEM, MEM_HEADS, MEM_HEADDIM), 1.0),
        "cache_mem_v": nrm(ks[7], (DEPTH, DEC_BATCH, N_MEM, MEM_HEADS, MEM_HEADDIM), 1.0),
        "norm_w": 1.0 + nrm(ks[8], (DEPTH, D_MODEL), 0.02),
        "w_in": nrm(ks[9], (DEPTH, D_MODEL, D_IN_TOTAL), D_MODEL ** -0.5),
        "conv_a_w": nrm(ks[10], (DEPTH, CONV_K_A, CONV_W), CONV_K_A ** -0.5),
        "w_a_out": nrm(ks[11], (DEPTH, CONV_W, D_MODEL), CONV_W ** -0.5),
        "conv_s_w": nrm(ks[12], (DEPTH, CONV_K_S, SSM_XBC), CONV_K_S ** -0.5),
        "conv_s_b": nrm(ks[13], (DEPTH, SSM_XBC), 0.01),
        "dt_bias": dt0 + jnp.log(-jnp.expm1(-dt0)),
        "a_log": jnp.log(jax.random.uniform(ks[15], (DEPTH, SSM_HEADS), minval=1.0, maxval=16.0)),
        "d_skip": 1.0 + nrm(ks[16], (DEPTH, SSM_HEADS), 0.1),
        "ssm_norm_w": 1.0 + nrm(ks[17], (DEPTH, SSM_INNER), 0.02),
        "w_s_out": nrm(ks[18], (DEPTH, SSM_INNER, D_MODEL), SSM_INNER ** -0.5),
        "mem_norm_w": 1.0 + nrm(ks[19], (DEPTH, D_MODEL), 0.02),
        "w_mem_kv": nrm(ks[20], (DEPTH, D_MODEL, 2 * ATTN_W), D_MODEL ** -0.5),
        "w_m_out": nrm(ks[21], (DEPTH, ATTN_W, D_MODEL), ATTN_W ** -0.5),
        "w_o": nrm(ks[22], (DEPTH, D_MODEL, D_MODEL), D_MODEL ** -0.5),
        "final_norm_w": 1.0 + nrm(ks[23], (D_MODEL,), 0.02),
    }


def reference(x_prompt, x_sample, mem_prompt, state_conv_a, state_conv_s, state_ssm, cache_mem_k, cache_mem_v,
              norm_w, w_in, conv_a_w, w_a_out, conv_s_w, conv_s_b, dt_bias, a_log, d_skip, ssm_norm_w,
              w_s_out, mem_norm_w, w_mem_kv, w_m_out, w_o, final_norm_w):
    bp = x_prompt.shape[0]
    dtp = x_prompt.dtype
    x = x_prompt
    pa, ps, ph, pk, pv = [], [], [], [], []
    for l in range(DEPTH):
        mk, mv = mem_kv(mem_prompt, mem_norm_w[l], w_mem_kv[l])
        buf_a = jnp.zeros((bp, CONV_K_A - 1, CONV_W), dtp)
        buf_s = jnp.zeros((bp, CONV_K_S - 1, SSM_XBC), dtp)
        h0 = jnp.zeros((bp, SSM_HEADS, SSM_HEADDIM, SSM_STATE), dtp)
        x, na, ns, nh = mixer_layer(x, mk, mv, buf_a, buf_s, h0, norm_w[l], w_in[l], conv_a_w[l], w_a_out[l],
                                    conv_s_w[l], conv_s_b[l], dt_bias[l], a_log[l], d_skip[l], ssm_norm_w[l],
                                    w_s_out[l], w_m_out[l], w_o[l])
        pa.append(na); ps.append(ns); ph.append(nh); pk.append(mk); pv.append(mv)
    y_prompt = rmsnorm(x, final_norm_w)
    x = x_sample
    sa, ss, sh = [], [], []
    for l in range(DEPTH):
        x, na, ns, nh = mixer_layer(x, cache_mem_k[l], cache_mem_v[l], state_conv_a[l], state_conv_s[l],
                                    state_ssm[l], norm_w[l], w_in[l], conv_a_w[l], w_a_out[l],
                                    conv_s_w[l], conv_s_b[l], dt_bias[l], a_log[l], d_skip[l], ssm_norm_w[l],
                                    w_s_out[l], w_m_out[l], w_o[l])
        sa.append(na); ss.append(ns); sh.append(nh)
    y_sample = rmsnorm(x, final_norm_w)
    return (y_prompt, y_sample,
            jnp.stack(pa), jnp.stack(ps), jnp.stack(ph), jnp.stack(pk), jnp.stack(pv),
            jnp.stack(sa), jnp.stack(ss), jnp.stack(sh))
```

```python
import functools

import jax
import jax.numpy as jnp
from jax import lax
from jax.experimental import pallas as pl
from jax.experimental.pallas import tpu as pltpu

F32 = jnp.float32
BF16 = jnp.bfloat16

D_MODEL = 1024
CONV_W = 1024
SSM_INNER = 2048
SSM_HEADDIM = 64
SSM_HEADS = 32
SSM_GROUPS = 4
SSM_STATE = 128
SSM_XBC = SSM_INNER + 2 * SSM_GROUPS * SSM_STATE
HEADS_PER_GROUP = SSM_HEADS // SSM_GROUPS
GROUP_W = SSM_INNER // SSM_GROUPS
N_PAIRS = SSM_HEADS // 2
N_MEM = 256
MEM_HEADS = 4
MEM_HEADDIM = 256
ATTN_W = MEM_HEADS * MEM_HEADDIM
EPS = 1e-6
SSD_CHUNK = 128
LANES = 128
SUBLANES = 8
VMEM_LIMIT = 52 * 1024 * 1024

_OFF_A = 0
_OFF_SZ = 4 * CONV_W
_OFF_SX = _OFF_SZ + SSM_INNER
_OFF_DT = _OFF_SX + SSM_XBC
_OFF_M = _OFF_DT + SSM_HEADS
_OFF_G = _OFF_M + 2 * ATTN_W


def _dot(a, b):
    return jnp.dot(a, b, preferred_element_type=F32)


def _dot_nt(a, b):
    return lax.dot_general(a, b, (((1,), (1,)), ((), ())), preferred_element_type=F32)


def _rms(x, w):
    return x * lax.rsqrt(jnp.mean(x * x, axis=-1, keepdims=True) + EPS) * w


def _sigmoid(x):
    return 0.5 * jnp.tanh(0.5 * x) + 0.5


def _silu(x):
    return x * _sigmoid(x)


def _softplus(x):
    return jnp.maximum(x, 0.0) + jnp.log1p(jnp.exp(-jnp.abs(x)))


def _softmax_rows(s):
    m = jnp.max(s, axis=-1, keepdims=True)
    e = jnp.exp(s - m)
    return e / jnp.sum(e, axis=-1, keepdims=True)


def _split3(x):
    hi = x.astype(BF16)
    r1 = x - hi.astype(F32)
    mid = r1.astype(BF16)
    lo = (r1 - mid.astype(F32)).astype(BF16)
    return hi, mid, lo


def _const(shape):
    return pl.BlockSpec(shape, lambda *_: (0,) * len(shape), pipeline_mode=pl.Buffered(1))


def _params(sem):
    return pltpu.CompilerParams(dimension_semantics=sem, vmem_limit_bytes=VMEM_LIMIT)


def _kv_kernel(mem_ref, nw_ref, w_ref, k_ref, v_ref, kt_ref, vb_ref):
    u = _rms(mem_ref[...], nw_ref[...]).astype(BF16)
    kv = _dot(u, w_ref[...])
    k = kv[:, :ATTN_W]
    v = kv[:, ATTN_W:]
    k_ref[...] = k
    v_ref[...] = v
    kt_ref[...] = k.T.astype(BF16)
    vb_ref[...] = v.astype(BF16)


def _mem_kv(mem, nw, w_kv):
    bt = mem.shape[0]
    blk = lambda r, c: pl.BlockSpec((None, r, c), lambda b: (b, 0, 0))
    return pl.pallas_call(
        _kv_kernel,
        grid=(bt,),
        in_specs=[blk(N_MEM, D_MODEL), _const((1, D_MODEL)), _const((D_MODEL, 2 * ATTN_W))],
        out_specs=[blk(N_MEM, ATTN_W), blk(N_MEM, ATTN_W), blk(ATTN_W, N_MEM), blk(N_MEM, ATTN_W)],
        out_shape=[jax.ShapeDtypeStruct((bt, N_MEM, ATTN_W), F32),
                   jax.ShapeDtypeStruct((bt, N_MEM, ATTN_W), F32),
                   jax.ShapeDtypeStruct((bt, ATTN_W, N_MEM), BF16),
                   jax.ShapeDtypeStruct((bt, N_MEM, ATTN_W), BF16)],
        compiler_params=_params(("arbitrary",)),
        name="prompt_kv",
    )(mem, nw, w_kv)


def _am_kernel(x_ref, nw_ref, wa_ref, wm_ref, wg_ref, caw_ref, wao_ref, wmo_ref, kt_ref, vb_ref,
               acc_ref, ca_ref, vbuf):
    c = pl.program_id(1)
    t = x_ref.shape[0]
    pad = SUBLANES

    @pl.when(c == 0)
    def _():
        vbuf[0:pad, :] = jnp.zeros((pad, CONV_W), F32)

    ub = _rms(x_ref[...], nw_ref[...]).astype(BF16)

    pa = _dot(ub, wa_ref[...])
    cb = pa[:, 0:CONV_W]
    v = pa[:, CONV_W:2 * CONV_W] * pa[:, 2 * CONV_W:3 * CONV_W]
    cz = pa[:, 3 * CONV_W:]
    vbuf[pad:pad + t, :] = v
    caw = caw_ref[...]
    conv = (caw[0:1] * vbuf[pad - 2:pad - 2 + t, :] + caw[1:2] * vbuf[pad - 1:pad - 1 + t, :]
            + caw[2:3] * v)
    ca_ref[...] = vbuf[pad + t - 2:pad + t, :]
    vbuf[0:pad, :] = vbuf[t:t + pad, :]
    y_a = _dot((_silu(cz) * cb * conv).astype(BF16), wao_ref[...])

    pm = _dot(ub, wm_ref[...])
    heads = []
    for h in range(MEM_HEADS):
        sl = slice(h * MEM_HEADDIM, (h + 1) * MEM_HEADDIM)
        s = _dot(pm[:, sl].astype(BF16), kt_ref[sl, :]) * (MEM_HEADDIM ** -0.5)
        p = _softmax_rows(s)
        heads.append(_dot(p.astype(BF16), vb_ref[:, sl]))
    o = jnp.concatenate(heads, axis=1)
    y_m = _dot((_silu(pm[:, ATTN_W:]) * o).astype(BF16), wmo_ref[...])

    pg = _dot(ub, wg_ref[...])
    acc_ref[...] = _sigmoid(pg[:, :D_MODEL]) * y_a + _sigmoid(pg[:, D_MODEL:]) * y_m


def _prompt_am(x, nw, wa, wm, wg_am, caw, wao, wmo, kt, vb, tile):
    bt, seq, _ = x.shape
    nc = seq // tile
    tok = lambda w: pl.BlockSpec((None, tile, w), lambda b, c: (b, c, 0))
    per_b = lambda r, w: pl.BlockSpec((None, r, w), lambda b, c: (b, 0, 0))
    return pl.pallas_call(
        _am_kernel,
        grid=(bt, nc),
        in_specs=[tok(D_MODEL), _const((1, D_MODEL)), _const((D_MODEL, 4 * CONV_W)),
                  _const((D_MODEL, 2 * ATTN_W)), _const((D_MODEL, 2 * D_MODEL)), _const((3, CONV_W)),
                  _const((CONV_W, D_MODEL)), _const((ATTN_W, D_MODEL)),
                  per_b(ATTN_W, N_MEM), per_b(N_MEM, ATTN_W)],
        out_specs=[tok(D_MODEL), per_b(2, CONV_W)],
        out_shape=[jax.ShapeDtypeStruct((bt, seq, D_MODEL), F32),
                   jax.ShapeDtypeStruct((bt, 2, CONV_W), F32)],
        scratch_shapes=[pltpu.VMEM((tile + SUBLANES, CONV_W), F32)],
        compiler_params=_params(("arbitrary", "arbitrary")),
        name="prompt_am",
    )(x, nw, wa, wm, wg_am, caw, wao, wmo, kt, vb)


def _cumsum_rows(da):
    q = da.shape[0]
    r = lax.broadcasted_iota(jnp.int32, (q, q), 0)
    c = lax.broadcasted_iota(jnp.int32, (q, q), 1)
    tri = jnp.where(r >= c, 1.0, 0.0).astype(BF16)
    hi, mid, lo = _split3(da)
    return _dot(tri, hi) + _dot(tri, mid) + _dot(tri, lo)


def _ssd_kernel(x_ref, acc_ref, nw_ref, wsz_ref, wsx_ref, wdt_ref, wgs_ref, csw_ref, csb_ref, dtb_ref,
                alog_ref, dsk_ref, snw_ref, wso_ref, wo_ref, fnw_ref,
                out_ref, cs_ref, ssm_ref,
                xbuf, xbc_scr, dt_scr, y_scr, ht_scr, *, final_norm):
    c = pl.program_id(1)
    t = x_ref.shape[0]
    q = SSD_CHUNK
    pad = SUBLANES

    @pl.when(c == 0)
    def _():
        xbuf[0:pad, :] = jnp.zeros((pad, SSM_XBC), F32)
        ht_scr[...] = jnp.zeros(ht_scr.shape, F32)

    x = x_ref[...]
    ub = _rms(x, nw_ref[...]).astype(BF16)

    sxbc = _dot(ub, wsx_ref[...])
    xbuf[pad:pad + t, :] = sxbc
    csw = csw_ref[...]
    yc = (csw[0:1] * xbuf[pad - 3:pad - 3 + t, :] + csw[1:2] * xbuf[pad - 2:pad - 2 + t, :]
          + csw[2:3] * xbuf[pad - 1:pad - 1 + t, :] + csw[3:4] * sxbc + csb_ref[...])
    cs_ref[...] = xbuf[pad + t - 3:pad + t, :]
    xbuf[0:pad, :] = xbuf[t:t + pad, :]
    xbc_scr[...] = _silu(yc)
    dt_scr[...] = _softplus(_dot(ub, wdt_ref[...]) + dtb_ref[...])
    a_row = -jnp.exp(alog_ref[...])

    rr = lax.broadcasted_iota(jnp.int32, (q, q), 0)
    cc = lax.broadcasted_iota(jnp.int32, (q, q), 1)
    tril = rr >= cc
    left = lax.broadcasted_iota(jnp.int32, (1, LANES), 1) < SSM_HEADDIM

    def chunk(j, carry):
        r0 = pl.multiple_of(j * q, q)
        rows = pl.ds(r0, q)
        dtj = dt_scr[rows, :]
        cs = _cumsum_rows(dtj * a_row)
        cs_t = cs.T
        dt_t = dtj.T
        w_t = dt_t * jnp.exp(cs_t[:, q - 1:q] - cs_t)
        etot = jnp.exp(cs[q - 1:q, :])

        scores, bm_t, cms = [], [], []
        for g in range(SSM_GROUPS):
            bg = xbc_scr[rows, SSM_INNER + g * SSM_STATE:SSM_INNER + (g + 1) * SSM_STATE]
            cg = xbc_scr[rows, SSM_INNER + (SSM_GROUPS + g) * SSM_STATE:
                         SSM_INNER + (SSM_GROUPS + g + 1) * SSM_STATE]
            scores.append(_dot_nt(cg.astype(BF16), bg.astype(BF16)))
            bm_t.append(bg.T)
            cms.append(cg)

        for pair in range(N_PAIRS):
            g = (2 * pair) // HEADS_PER_GROUP
            lanes = slice(pair * LANES, (pair + 1) * LANES)
            lhs, wbs = [], []
            for h in (2 * pair, 2 * pair + 1):
                col = jnp.broadcast_to(cs[:, h:h + 1], (q, q))
                row = jnp.broadcast_to(cs_t[h:h + 1, :], (q, q))
                lmat = jnp.exp(jnp.where(tril, col - row, -jnp.inf))
                lhs.append((scores[g] * lmat * dt_t[h:h + 1, :]).astype(BF16))
                lhs.append((cms[g] * jnp.exp(col)).astype(BF16))
                wbs.append((bm_t[g] * w_t[h:h + 1, :]).astype(BF16))
            xs_p = xbc_scr[rows, lanes]
            ht = ht_scr[pair]
            rhs = jnp.concatenate([jnp.where(left, xs_p, 0.0), jnp.where(left, ht, 0.0),
                                   jnp.where(left, 0.0, xs_p), jnp.where(left, 0.0, ht)], axis=0)
            yp = _dot(jnp.concatenate(lhs, axis=1), rhs.astype(BF16))
            y_scr[rows, lanes] = yp + dsk_ref[:, lanes] * xs_p
            xsb = xs_p.astype(BF16)
            upd = jnp.where(left, _dot(wbs[0], xsb), _dot(wbs[1], xsb))
            dec = jnp.where(left, jnp.broadcast_to(etot[:, 2 * pair:2 * pair + 1], (1, LANES)),
                            jnp.broadcast_to(etot[:, 2 * pair + 1:2 * pair + 2], (1, LANES)))
            ht_scr[pair] = ht * dec + upd
        return carry

    lax.fori_loop(0, t // q, chunk, 0)

    @pl.when(c == pl.num_programs(1) - 1)
    def _():
        for pair in range(N_PAIRS):
            ssm_ref[pair * LANES:(pair + 1) * LANES, :] = ht_scr[pair].T

    gz = y_scr[...] * _silu(_dot(ub, wsz_ref[...]))
    parts = []
    for g in range(SSM_GROUPS):
        gg = gz[:, g * GROUP_W:(g + 1) * GROUP_W]
        parts.append(gg * lax.rsqrt(jnp.mean(gg * gg, axis=-1, keepdims=True) + EPS))
    gn = jnp.concatenate(parts, axis=1) * snw_ref[...]
    y_s = _dot(gn.astype(BF16), wso_ref[...])
    merged = acc_ref[...] + _sigmoid(_dot(ub, wgs_ref[...])) * y_s
    xn = x + _dot(merged.astype(BF16), wo_ref[...])
    out_ref[...] = _rms(xn, fnw_ref[...]) if final_norm else xn


def _prompt_ssd(x, acc, nw, wsz, wsx, wdt, wgs, csw, csb, dtb, alog, dsk, snw, wso, wo, fnw, tile, final_norm):
    bt, seq, _ = x.shape
    nc = seq // tile
    tok = lambda w: pl.BlockSpec((None, tile, w), lambda b, c: (b, c, 0))
    per_b = lambda r, w: pl.BlockSpec((None, r, w), lambda b, c: (b, 0, 0))
    return pl.pallas_call(
        functools.partial(_ssd_kernel, final_norm=final_norm),
        grid=(bt, nc),
        in_specs=[tok(D_MODEL), tok(D_MODEL), _const((1, D_MODEL)), _const((D_MODEL, SSM_INNER)),
                  _const((D_MODEL, SSM_XBC)), _const((D_MODEL, LANES)), _const((D_MODEL, D_MODEL)),
                  _const((4, SSM_XBC)), _const((1, SSM_XBC)), _const((1, LANES)), _const((1, LANES)),
                  _const((1, SSM_INNER)), _const((1, SSM_INNER)), _const((SSM_INNER, D_MODEL)),
                  _const((D_MODEL, D_MODEL)), _const((1, D_MODEL))],
        out_specs=[tok(D_MODEL), per_b(3, SSM_XBC), per_b(SSM_HEADS * SSM_HEADDIM, SSM_STATE)],
        out_shape=[jax.ShapeDtypeStruct((bt, seq, D_MODEL), F32),
                   jax.ShapeDtypeStruct((bt, 3, SSM_XBC), F32),
                   jax.ShapeDtypeStruct((bt, SSM_HEADS * SSM_HEADDIM, SSM_STATE), F32)],
        scratch_shapes=[pltpu.VMEM((tile + SUBLANES, SSM_XBC), F32),
                        pltpu.VMEM((tile, SSM_XBC), F32),
                        pltpu.VMEM((tile, LANES), F32),
                        pltpu.VMEM((tile, SSM_INNER), F32),
                        pltpu.VMEM((N_PAIRS, SSM_STATE, LANES), F32)],
        compiler_params=_params(("arbitrary", "arbitrary")),
        name="prompt_ssd",
    )(x, acc, nw, wsz, wsx, wdt, wgs, csw, csb, dtb, alog, dsk, snw, wso, wo, fnw)


def _proj_kernel(x_ref, nw_ref, w_ref, o_ref):
    o_ref[...] = _dot(_rms(x_ref[...], nw_ref[...]).astype(BF16), w_ref[...])


def _sample_proj(x, nw, w_all, tn):
    m = x.shape[0]
    n = w_all.shape[1]
    return pl.pallas_call(
        _proj_kernel,
        grid=(n // tn,),
        in_specs=[_const((m, D_MODEL)), _const((1, D_MODEL)), pl.BlockSpec((D_MODEL, tn), lambda j: (0, j))],
        out_specs=pl.BlockSpec((m, tn), lambda j: (0, j)),
        out_shape=jax.ShapeDtypeStruct((m, n), F32),
        compiler_params=_params(("arbitrary",)),
        name="sample_proj",
    )(x, nw, w_all)


_S_A = 0
_S_SZ = 4 * CONV_W
_S_SX = _S_SZ + SSM_INNER
_S_DT = _S_SX + SSM_XBC
_S_M = _S_DT + SSM_INNER
_S_G = _S_M + 2 * ATTN_W
_S_END = _S_G + 3 * D_MODEL
_R_DA = SSM_INNER
_R_B = 2 * SSM_INNER
_R_C = _R_B + SSM_GROUPS * SSM_STATE
_R_END = _R_C + SSM_GROUPS * SSM_STATE


def _pre_kernel(p_ref, sca_ref, scs_ref, caw_ref, csw_ref, csb_ref, dtb_ref, alog_ref,
                acta_ref, row_ref, xs_ref, nca_ref, ncs_ref):
    cb = p_ref[:, _S_A:_S_A + CONV_W]
    v = p_ref[:, _S_A + CONV_W:_S_A + 2 * CONV_W] * p_ref[:, _S_A + 2 * CONV_W:_S_A + 3 * CONV_W]
    cz = p_ref[:, _S_A + 3 * CONV_W:_S_A + 4 * CONV_W]
    caw = caw_ref[...]
    b0 = sca_ref[:, 0:CONV_W]
    b1 = sca_ref[:, CONV_W:2 * CONV_W]
    conv = caw[0:1] * b0 + caw[1:2] * b1 + caw[2:3] * v
    acta_ref[...] = _silu(cz) * cb * conv
    nca_ref[:, 0:CONV_W] = b1
    nca_ref[:, CONV_W:2 * CONV_W] = v

    sxbc = p_ref[:, _S_SX:_S_SX + SSM_XBC]
    csw = csw_ref[...]
    s0 = scs_ref[:, 0:SSM_XBC]
    s1 = scs_ref[:, SSM_XBC:2 * SSM_XBC]
    s2 = scs_ref[:, 2 * SSM_XBC:3 * SSM_XBC]
    xbc = _silu(csw[0:1] * s0 + csw[1:2] * s1 + csw[2:3] * s2 + csw[3:4] * sxbc + csb_ref[...])
    ncs_ref[:, 0:SSM_XBC] = s1
    ncs_ref[:, SSM_XBC:2 * SSM_XBC] = s2
    ncs_ref[:, 2 * SSM_XBC:3 * SSM_XBC] = sxbc
    xs = xbc[:, :SSM_INNER]
    dt = _softplus(p_ref[:, _S_DT:_S_DT + SSM_INNER] + dtb_ref[...])
    xs_ref[...] = xs
    row_ref[:, 0:_R_DA] = xs * dt
    row_ref[:, _R_DA:_R_B] = jnp.exp(dt * -jnp.exp(alog_ref[...]))
    row_ref[:, _R_B:_R_END] = xbc[:, SSM_INNER:]


def _sample_pre(proj, sca, scs, caw, csw, csb, dtb_e, alog_e, rb):
    m = proj.shape[0]
    rows = lambda w: pl.BlockSpec((rb, w), lambda i: (i, 0))
    return pl.pallas_call(
        _pre_kernel,
        grid=(m // rb,),
        in_specs=[rows(_S_END), rows(2 * CONV_W), rows(3 * SSM_XBC), _const((3, CONV_W)), _const((4, SSM_XBC)),
                  _const((1, SSM_XBC)), _const((1, SSM_INNER)), _const((1, SSM_INNER))],
        out_specs=[rows(CONV_W), rows(_R_END), rows(SSM_INNER), rows(2 * CONV_W), rows(3 * SSM_XBC)],
        out_shape=[jax.ShapeDtypeStruct((m, CONV_W), F32),
                   jax.ShapeDtypeStruct((m, _R_END), F32),
                   jax.ShapeDtypeStruct((m, SSM_INNER), F32),
                   jax.ShapeDtypeStruct((m, 2 * CONV_W), F32),
                   jax.ShapeDtypeStruct((m, 3 * SSM_XBC), F32)],
        compiler_params=_params(("arbitrary",)),
        name="sample_pre",
    )(proj, sca, scs, caw, csw, csb, dtb_e, alog_e)


def _sssd_kernel(row_ref, h0_ref, hn_ref, y_ref):
    nb = row_ref.shape[0]
    sub_x = lax.broadcasted_iota(jnp.int32, (LANES, GROUP_W), 0)
    sub_r = lax.broadcasted_iota(jnp.int32, (LANES, 2 * SSM_STATE), 0)
    sub_c = lax.broadcasted_iota(jnp.int32, (SUBLANES, SSM_STATE), 0)
    for i in range(nb):
        row = row_ref[i]
        ys = []
        for g in range(SSM_GROUPS):
            ch = slice(g * GROUP_W, (g + 1) * GROUP_W)
            xdt = row[:, ch]
            x_hi = xdt.astype(BF16).astype(F32)
            x_lo = xdt - x_hi
            da = row[:, _R_DA + g * GROUP_W:_R_DA + (g + 1) * GROUP_W]
            a_hi = da.astype(BF16).astype(F32)
            a_mid = (da - a_hi).astype(BF16).astype(F32)
            a_lo = da - a_hi - a_mid
            xk = jnp.where((sub_x == 0) | (sub_x == 2), x_hi,
                           jnp.where((sub_x == 1) | (sub_x == 3), x_lo,
                                     jnp.where(sub_x == 4, a_hi,
                                               jnp.where(sub_x == 5, a_mid,
                                                         jnp.where(sub_x == 6, a_lo, 0.0)))))
            bg = row[:, _R_B + g * SSM_STATE:_R_B + (g + 1) * SSM_STATE]
            b_hi = bg.astype(BF16).astype(F32)
            b_lo = bg - b_hi
            zero = jnp.zeros((1, SSM_STATE), F32)
            rk = jnp.where(sub_r < 2, jnp.concatenate([b_hi, zero], axis=1),
                           jnp.where(sub_r < 4, jnp.concatenate([b_lo, zero], axis=1),
                                     jnp.where(sub_r < 7, jnp.concatenate([zero, zero + 1.0], axis=1), 0.0)))
            bc = _dot(xk.T.astype(BF16), rk.astype(BF16))
            hnew = bc[:, SSM_STATE:] * h0_ref[i, ch, :] + bc[:, :SSM_STATE]
            hn_ref[i, ch, :] = hnew
            cg = row[:, _R_C + g * SSM_STATE:_R_C + (g + 1) * SSM_STATE]
            c8 = jnp.where(sub_c == 0, cg, 0.0).astype(BF16)
            ys.append(_dot_nt(c8, hnew.astype(BF16))[0:1])
        y_ref[i] = jnp.concatenate(ys, axis=1)


def _sample_ssd(rows, h0, nb):
    m = rows.shape[0]
    hp = SSM_HEADS * SSM_HEADDIM
    return pl.pallas_call(
        _sssd_kernel,
        grid=(m // nb,),
        in_specs=[pl.BlockSpec((nb, 1, _R_END), lambda i: (i, 0, 0)),
                  pl.BlockSpec((nb, hp, SSM_STATE), lambda i: (i, 0, 0))],
        out_specs=[pl.BlockSpec((nb, hp, SSM_STATE), lambda i: (i, 0, 0)),
                   pl.BlockSpec((nb, 1, hp), lambda i: (i, 0, 0))],
        out_shape=[jax.ShapeDtypeStruct((m, hp, SSM_STATE), F32),
                   jax.ShapeDtypeStruct((m, 1, hp), F32)],
        compiler_params=_params(("arbitrary",)),
        name="sample_ssd",
    )(rows.reshape(m, 1, _R_END), h0)


def _sattn_kernel(q_ref, k_ref, v_ref, o_ref):
    nb = q_ref.shape[0]
    sub = lax.broadcasted_iota(jnp.int32, (SUBLANES, ATTN_W), 0)
    lane = lax.broadcasted_iota(jnp.int32, (SUBLANES, ATTN_W), 1)
    own = (lane >= sub * MEM_HEADDIM) & (lane < (sub + 1) * MEM_HEADDIM)
    for i in range(nb):
        q8 = jnp.where(own, q_ref[i], 0.0).astype(BF16)
        s = _dot_nt(q8, k_ref[i].astype(BF16)) * (MEM_HEADDIM ** -0.5)
        p = _softmax_rows(s)
        o8 = _dot(p.astype(BF16), v_ref[i].astype(BF16))
        o_ref[i] = jnp.sum(jnp.where(own, o8, 0.0), axis=0, keepdims=True)


def _sample_attn(q, k, v, nb):
    m = q.shape[0]
    return pl.pallas_call(
        _sattn_kernel,
        grid=(m // nb,),
        in_specs=[pl.BlockSpec((nb, 1, ATTN_W), lambda i: (i, 0, 0)),
                  pl.BlockSpec((nb, N_MEM, ATTN_W), lambda i: (i, 0, 0)),
                  pl.BlockSpec((nb, N_MEM, ATTN_W), lambda i: (i, 0, 0))],
        out_specs=pl.BlockSpec((nb, 1, ATTN_W), lambda i: (i, 0, 0)),
        out_shape=jax.ShapeDtypeStruct((m, 1, ATTN_W), F32),
        compiler_params=_params(("arbitrary",)),
        name="sample_attn",
    )(q.reshape(m, 1, ATTN_W), k, v)


def _post_kernel(x_ref, p_ref, acta_ref, xs_ref, ysd_ref, o_ref, dsk_ref, snw_ref, wao_ref, wso_ref, wmo_ref,
                 wo_ref, fnw_ref, out_ref, *, final_norm):
    y_a = _dot(acta_ref[...].astype(BF16), wao_ref[...])
    gz = (ysd_ref[...] + dsk_ref[...] * xs_ref[...]) * _silu(p_ref[:, _S_SZ:_S_SZ + SSM_INNER])
    parts = []
    for g in range(SSM_GROUPS):
        gg = gz[:, g * GROUP_W:(g + 1) * GROUP_W]
        parts.append(gg * lax.rsqrt(jnp.mean(gg * gg, axis=-1, keepdims=True) + EPS))
    gn = jnp.concatenate(parts, axis=1) * snw_ref[...]
    y_s = _dot(gn.astype(BF16), wso_ref[...])
    mz = p_ref[:, _S_M + ATTN_W:_S_M + 2 * ATTN_W]
    y_m = _dot((_silu(mz) * o_ref[...]).astype(BF16), wmo_ref[...])
    merged = (_sigmoid(p_ref[:, _S_G:_S_G + D_MODEL]) * y_a
              + _sigmoid(p_ref[:, _S_G + D_MODEL:_S_G + 2 * D_MODEL]) * y_s
              + _sigmoid(p_ref[:, _S_G + 2 * D_MODEL:_S_G + 3 * D_MODEL]) * y_m)
    xn = x_ref[...] + _dot(merged.astype(BF16), wo_ref[...])
    out_ref[...] = _rms(xn, fnw_ref[...]) if final_norm else xn


def _sample_post(x, proj, acta, xs, ysd, o, dsk_e, snw, wao, wso, wmo, wo, fnw, final_norm):
    m = x.shape[0]
    full = lambda a: _const(a.shape)
    args = (x, proj, acta, xs, ysd, o, dsk_e, snw, wao, wso, wmo, wo, fnw)
    return pl.pallas_call(
        functools.partial(_post_kernel, final_norm=final_norm),
        grid=(1,),
        in_specs=[full(a) for a in args],
        out_specs=_const((m, D_MODEL)),
        out_shape=jax.ShapeDtypeStruct((m, D_MODEL), F32),
        compiler_params=_params(("arbitrary",)),
        name="sample_post",
    )(*args)


def _layer_weights(l, norm_w, w_in, conv_a_w, w_a_out, conv_s_w, conv_s_b, dt_bias, a_log, d_skip, ssm_norm_w,
                   w_s_out, mem_norm_w, w_mem_kv, w_m_out, w_o):
    wb = w_in[l].astype(BF16)
    w = {}
    w["nw"] = norm_w[l][None]
    w["wa"] = wb[:, _OFF_A:_OFF_SZ]
    w["wsz"] = wb[:, _OFF_SZ:_OFF_SX]
    w["wsx"] = wb[:, _OFF_SX:_OFF_DT]
    wdt = wb[:, _OFF_DT:_OFF_M]
    w["wdt"] = jnp.pad(wdt, ((0, 0), (0, LANES - SSM_HEADS)))
    w["wdt_e"] = jnp.repeat(wdt, SSM_HEADDIM, axis=1)
    w["wm"] = wb[:, _OFF_M:_OFF_G]
    wg = wb[:, _OFF_G:]
    w["wg"] = wg
    w["wg_am"] = jnp.concatenate([wg[:, :D_MODEL], wg[:, 2 * D_MODEL:]], axis=1)
    w["wg_s"] = wg[:, D_MODEL:2 * D_MODEL]
    w["caw"] = conv_a_w[l]
    w["wao"] = w_a_out[l].astype(BF16)
    w["csw"] = conv_s_w[l]
    w["csb"] = conv_s_b[l][None]
    w["dtb"] = jnp.pad(dt_bias[l], (0, LANES - SSM_HEADS))[None]
    w["alog"] = jnp.pad(a_log[l], (0, LANES - SSM_HEADS))[None]
    w["dtb_e"] = jnp.repeat(dt_bias[l], SSM_HEADDIM)[None]
    w["alog_e"] = jnp.repeat(a_log[l], SSM_HEADDIM)[None]
    w["dsk_e"] = jnp.repeat(d_skip[l], SSM_HEADDIM)[None]
    w["snw"] = ssm_norm_w[l][None]
    w["wso"] = w_s_out[l].astype(BF16)
    w["mnw"] = mem_norm_w[l][None]
    w["wkv"] = w_mem_kv[l].astype(BF16)
    w["wmo"] = w_m_out[l].astype(BF16)
    w["wo"] = w_o[l].astype(BF16)
    w["w_all"] = jnp.concatenate([w["wa"], w["wsz"], w["wsx"], w["wdt_e"], w["wm"], wg], axis=1)
    return w


def _prompt_tile(seq):
    for t in (256, 128):
        if seq % t == 0:
            return t
    raise ValueError(f"sequence length {seq} must be a multiple of {SSD_CHUNK}")


def kernel(x_prompt, x_sample, mem_prompt, state_conv_a, state_conv_s, state_ssm, cache_mem_k, cache_mem_v,
           norm_w, w_in, conv_a_w, w_a_out, conv_s_w, conv_s_b, dt_bias, a_log, d_skip, ssm_norm_w,
           w_s_out, mem_norm_w, w_mem_kv, w_m_out, w_o, final_norm_w):
    depth = w_in.shape[0]
    bp, seq, _ = x_prompt.shape
    bs = x_sample.shape[0]
    assert x_sample.shape[1] == 1
    fnw = final_norm_w[None]
    tile = _prompt_tile(seq)
    weights = [_layer_weights(l, norm_w, w_in, conv_a_w, w_a_out, conv_s_w, conv_s_b, dt_bias, a_log, d_skip,
                              ssm_norm_w, w_s_out, mem_norm_w, w_mem_kv, w_m_out, w_o) for l in range(depth)]

    x = x_prompt
    pa, ps, ph, pk, pv = [], [], [], [], []
    for l, w in enumerate(weights):
        last = l == depth - 1
        k, v, kt, vb = _mem_kv(mem_prompt, w["mnw"], w["wkv"])
        acc, ca = _prompt_am(x, w["nw"], w["wa"], w["wm"], w["wg_am"], w["caw"], w["wao"], w["wmo"], kt, vb, tile)
        x, cs, hs = _prompt_ssd(x, acc, w["nw"], w["wsz"], w["wsx"], w["wdt"], w["wg_s"], w["csw"], w["csb"],
                                w["dtb"], w["alog"], w["dsk_e"], w["snw"], w["wso"], w["wo"], fnw, tile, last)
        pa.append(ca)
        ps.append(cs)
        ph.append(hs.reshape(bp, SSM_HEADS, SSM_HEADDIM, SSM_STATE))
        pk.append(k.reshape(bp, N_MEM, MEM_HEADS, MEM_HEADDIM))
        pv.append(v.reshape(bp, N_MEM, MEM_HEADS, MEM_HEADDIM))
    y_prompt = x

    x = x_sample.reshape(bs, D_MODEL)
    sa, ss, sh = [], [], []
    for l, w in enumerate(weights):
        last = l == depth - 1
        proj = _sample_proj(x, w["nw"], w["w_all"], 2048)
        acta, rows, xs, nca, ncs = _sample_pre(
            proj, state_conv_a[l].reshape(bs, 2 * CONV_W), state_conv_s[l].reshape(bs, 3 * SSM_XBC),
            w["caw"], w["csw"], w["csb"], w["dtb_e"], w["alog_e"], 32)
        hn, ysd = _sample_ssd(rows, state_ssm[l].reshape(bs, SSM_HEADS * SSM_HEADDIM, SSM_STATE), 2)
        o = _sample_attn(proj[:, _S_M:_S_M + ATTN_W], cache_mem_k[l].reshape(bs, N_MEM, ATTN_W),
                         cache_mem_v[l].reshape(bs, N_MEM, ATTN_W), 2)
        x = _sample_post(x, proj, acta, xs, ysd.reshape(bs, SSM_INNER), o.reshape(bs, ATTN_W), w["dsk_e"],
                         w["snw"], w["wao"], w["wso"], w["wmo"], w["wo"], fnw, last)
        sa.append(nca.reshape(bs, 2, CONV_W))
        ss.append(ncs.reshape(bs, 3, SSM_XBC))
        sh.append(hn.reshape(bs, SSM_HEADS, SSM_HEADDIM, SSM_STATE))
    y_sample = x.reshape(bs, 1, D_MODEL)

    return (y_prompt, y_sample, jnp.stack(pa), jnp.stack(ps), jnp.stack(ph), jnp.stack(pk), jnp.stack(pv),
            jnp.stack(sa), jnp.stack(ss), jnp.stack(sh))
```

```python
import functools

import jax
import jax.numpy as jnp
from jax import lax
from jax.experimental import pallas as pl
from jax.experimental.pallas import tpu as pltpu

F32 = jnp.float32
BF16 = jnp.bfloat16

D_MODEL = 1024
CONV_W = 1024
SSM_INNER = 2048
SSM_HEADDIM = 64
SSM_HEADS = 32
SSM_GROUPS = 4
SSM_STATE = 128
SSM_XBC = SSM_INNER + 2 * SSM_GROUPS * SSM_STATE
HEADS_PER_GROUP = SSM_HEADS // SSM_GROUPS
GROUP_W = SSM_INNER // SSM_GROUPS
N_PAIRS = SSM_HEADS // 2
N_MEM = 256
MEM_HEADS = 4
MEM_HEADDIM = 256
ATTN_W = MEM_HEADS * MEM_HEADDIM
CACHE_ROWS = N_MEM * ATTN_W // 128
EPS = 1e-6
SSD_CHUNK = 128
LANES = 128
SUBLANES = 8
VMEM_LIMIT = 52 * 1024 * 1024

_OFF_A = 0
_OFF_SZ = 4 * CONV_W
_OFF_SX = _OFF_SZ + SSM_INNER
_OFF_DT = _OFF_SX + SSM_XBC
_OFF_M = _OFF_DT + SSM_HEADS
_OFF_G = _OFF_M + 2 * ATTN_W


def _dot(a, b):
    return jnp.dot(a, b, preferred_element_type=F32)


def _dot_nt(a, b):
    return lax.dot_general(a, b, (((1,), (1,)), ((), ())), preferred_element_type=F32)


def _rms(x, w):
    return x * lax.rsqrt(jnp.mean(x * x, axis=-1, keepdims=True) + EPS) * w


def _sigmoid(x):
    return 0.5 * jnp.tanh(0.5 * x) + 0.5


def _silu(x):
    return x * _sigmoid(x)


def _softplus(x):
    return jnp.maximum(x, 0.0) + jnp.log1p(jnp.exp(-jnp.abs(x)))


def _softmax_rows(s):
    m = jnp.max(s, axis=-1, keepdims=True)
    e = jnp.exp(s - m)
    return e / jnp.sum(e, axis=-1, keepdims=True)


def _split3(x):
    hi = x.astype(BF16)
    r1 = x - hi.astype(F32)
    mid = r1.astype(BF16)
    lo = (r1 - mid.astype(F32)).astype(BF16)
    return hi, mid, lo


def _const(shape):
    return pl.BlockSpec(shape, lambda *_: (0,) * len(shape), pipeline_mode=pl.Buffered(1))


def _params(sem):
    return pltpu.CompilerParams(dimension_semantics=sem, vmem_limit_bytes=VMEM_LIMIT)


def _kv_kernel(mem_ref, nw_ref, w_ref, k_ref, v_ref, kt_ref, vb_ref):
    u = _rms(mem_ref[...], nw_ref[...]).astype(BF16)
    kv = _dot(u, w_ref[...])
    k = kv[:, :ATTN_W]
    v = kv[:, ATTN_W:]
    k_ref[...] = k
    v_ref[...] = v
    kt_ref[...] = k.T.astype(BF16)
    vb_ref[...] = v.astype(BF16)


def _mem_kv(mem, nw, w_kv):
    bt = mem.shape[0]
    blk = lambda r, c: pl.BlockSpec((None, r, c), lambda b: (b, 0, 0))
    return pl.pallas_call(
        _kv_kernel,
        grid=(bt,),
        in_specs=[blk(N_MEM, D_MODEL), _const((1, D_MODEL)), _const((D_MODEL, 2 * ATTN_W))],
        out_specs=[blk(N_MEM, ATTN_W), blk(N_MEM, ATTN_W), blk(ATTN_W, N_MEM), blk(N_MEM, ATTN_W)],
        out_shape=[jax.ShapeDtypeStruct((bt, N_MEM, ATTN_W), F32),
                   jax.ShapeDtypeStruct((bt, N_MEM, ATTN_W), F32),
                   jax.ShapeDtypeStruct((bt, ATTN_W, N_MEM), BF16),
                   jax.ShapeDtypeStruct((bt, N_MEM, ATTN_W), BF16)],
        compiler_params=_params(("arbitrary",)),
        name="prompt_kv",
    )(mem, nw, w_kv)


def _am_kernel(x_ref, nw_ref, wa_ref, wm_ref, wg_ref, caw_ref, wao_ref, wmo_ref, kt_ref, vb_ref,
               acc_ref, ca_ref, vbuf):
    c = pl.program_id(1)
    t = x_ref.shape[0]
    pad = SUBLANES

    @pl.when(c == 0)
    def _():
        vbuf[0:pad, :] = jnp.zeros((pad, CONV_W), F32)

    ub = _rms(x_ref[...], nw_ref[...]).astype(BF16)

    pa = _dot(ub, wa_ref[...])
    cb = pa[:, 0:CONV_W]
    v = pa[:, CONV_W:2 * CONV_W] * pa[:, 2 * CONV_W:3 * CONV_W]
    cz = pa[:, 3 * CONV_W:]
    vbuf[pad:pad + t, :] = v
    caw = caw_ref[...]
    conv = (caw[0:1] * vbuf[pad - 2:pad - 2 + t, :] + caw[1:2] * vbuf[pad - 1:pad - 1 + t, :]
            + caw[2:3] * v)
    ca_ref[...] = vbuf[pad + t - 2:pad + t, :]
    vbuf[0:pad, :] = vbuf[t:t + pad, :]
    y_a = _dot((_silu(cz) * cb * conv).astype(BF16), wao_ref[...])

    pm = _dot(ub, wm_ref[...])
    heads = []
    for h in range(MEM_HEADS):
        sl = slice(h * MEM_HEADDIM, (h + 1) * MEM_HEADDIM)
        s = _dot(pm[:, sl].astype(BF16), kt_ref[sl, :]) * (MEM_HEADDIM ** -0.5)
        p = _softmax_rows(s)
        heads.append(_dot(p.astype(BF16), vb_ref[:, sl]))
    o = jnp.concatenate(heads, axis=1)
    y_m = _dot((_silu(pm[:, ATTN_W:]) * o).astype(BF16), wmo_ref[...])

    pg = _dot(ub, wg_ref[...])
    acc_ref[...] = _sigmoid(pg[:, :D_MODEL]) * y_a + _sigmoid(pg[:, D_MODEL:]) * y_m


def _prompt_am(x, nw, wa, wm, wg_am, caw, wao, wmo, kt, vb, tile):
    bt, seq, _ = x.shape
    nc = seq // tile
    tok = lambda w: pl.BlockSpec((None, tile, w), lambda b, c: (b, c, 0))
    per_b = lambda r, w: pl.BlockSpec((None, r, w), lambda b, c: (b, 0, 0))
    return pl.pallas_call(
        _am_kernel,
        grid=(bt, nc),
        in_specs=[tok(D_MODEL), _const((1, D_MODEL)), _const((D_MODEL, 4 * CONV_W)),
                  _const((D_MODEL, 2 * ATTN_W)), _const((D_MODEL, 2 * D_MODEL)), _const((3, CONV_W)),
                  _const((CONV_W, D_MODEL)), _const((ATTN_W, D_MODEL)),
                  per_b(ATTN_W, N_MEM), per_b(N_MEM, ATTN_W)],
        out_specs=[tok(D_MODEL), per_b(2, CONV_W)],
        out_shape=[jax.ShapeDtypeStruct((bt, seq, D_MODEL), F32),
                   jax.ShapeDtypeStruct((bt, 2, CONV_W), F32)],
        scratch_shapes=[pltpu.VMEM((tile + SUBLANES, CONV_W), F32)],
        compiler_params=_params(("arbitrary", "arbitrary")),
        name="prompt_am",
    )(x, nw, wa, wm, wg_am, caw, wao, wmo, kt, vb)


def _cumsum_rows(da):
    q = da.shape[0]
    r = lax.broadcasted_iota(jnp.int32, (q, q), 0)
    c = lax.broadcasted_iota(jnp.int32, (q, q), 1)
    tri = jnp.where(r >= c, 1.0, 0.0).astype(BF16)
    hi, mid, lo = _split3(da)
    return _dot(tri, hi) + _dot(tri, mid) + _dot(tri, lo)


def _ssd_kernel(x_ref, acc_ref, nw_ref, wsz_ref, wsx_ref, wdt_ref, wgs_ref, csw_ref, csb_ref, dtb_ref,
                alog_ref, dsk_ref, snw_ref, wso_ref, wo_ref, fnw_ref,
                out_ref, cs_ref, ssm_ref,
                xbuf, xbc_scr, dt_scr, y_scr, ht_scr, *, final_norm):
    c = pl.program_id(1)
    t = x_ref.shape[0]
    q = SSD_CHUNK
    pad = SUBLANES

    @pl.when(c == 0)
    def _():
        xbuf[0:pad, :] = jnp.zeros((pad, SSM_XBC), F32)
        ht_scr[...] = jnp.zeros(ht_scr.shape, F32)

    x = x_ref[...]
    ub = _rms(x, nw_ref[...]).astype(BF16)

    sxbc = _dot(ub, wsx_ref[...])
    xbuf[pad:pad + t, :] = sxbc
    csw = csw_ref[...]
    yc = (csw[0:1] * xbuf[pad - 3:pad - 3 + t, :] + csw[1:2] * xbuf[pad - 2:pad - 2 + t, :]
          + csw[2:3] * xbuf[pad - 1:pad - 1 + t, :] + csw[3:4] * sxbc + csb_ref[...])
    cs_ref[...] = xbuf[pad + t - 3:pad + t, :]
    xbuf[0:pad, :] = xbuf[t:t + pad, :]
    xbc_scr[...] = _silu(yc)
    dt_scr[...] = _softplus(_dot(ub, wdt_ref[...]) + dtb_ref[...])
    a_row = -jnp.exp(alog_ref[...])

    rr = lax.broadcasted_iota(jnp.int32, (q, q), 0)
    cc = lax.broadcasted_iota(jnp.int32, (q, q), 1)
    tril = rr >= cc
    left = lax.broadcasted_iota(jnp.int32, (1, LANES), 1) < SSM_HEADDIM

    def chunk(j, carry):
        r0 = pl.multiple_of(j * q, q)
        rows = pl.ds(r0, q)
        dtj = dt_scr[rows, :]
        cs = _cumsum_rows(dtj * a_row)
        cs_t = cs.T
        dt_t = dtj.T
        w_t = dt_t * jnp.exp(cs_t[:, q - 1:q] - cs_t)
        etot = jnp.exp(cs[q - 1:q, :])

        scores, bm_t, cms = [], [], []
        for g in range(SSM_GROUPS):
            bg = xbc_scr[rows, SSM_INNER + g * SSM_STATE:SSM_INNER + (g + 1) * SSM_STATE]
            cg = xbc_scr[rows, SSM_INNER + (SSM_GROUPS + g) * SSM_STATE:
                         SSM_INNER + (SSM_GROUPS + g + 1) * SSM_STATE]
            scores.append(_dot_nt(cg.astype(BF16), bg.astype(BF16)))
            bm_t.append(bg.T)
            cms.append(cg)

        for pair in range(N_PAIRS):
            g = (2 * pair) // HEADS_PER_GROUP
            lanes = slice(pair * LANES, (pair + 1) * LANES)
            lhs, wbs = [], []
            for h in (2 * pair, 2 * pair + 1):
                col = jnp.broadcast_to(cs[:, h:h + 1], (q, q))
                row = jnp.broadcast_to(cs_t[h:h + 1, :], (q, q))
                lmat = jnp.exp(jnp.where(tril, col - row, -jnp.inf))
                lhs.append((scores[g] * lmat * dt_t[h:h + 1, :]).astype(BF16))
                lhs.append((cms[g] * jnp.exp(col)).astype(BF16))
                wbs.append((bm_t[g] * w_t[h:h + 1, :]).astype(BF16))
            xs_p = xbc_scr[rows, lanes]
            ht = ht_scr[pair]
            rhs = jnp.concatenate([jnp.where(left, xs_p, 0.0), jnp.where(left, ht, 0.0),
                                   jnp.where(left, 0.0, xs_p), jnp.where(left, 0.0, ht)], axis=0)
            yp = _dot(jnp.concatenate(lhs, axis=1), rhs.astype(BF16))
            y_scr[rows, lanes] = yp + dsk_ref[:, lanes] * xs_p
            xsb = xs_p.astype(BF16)
            upd = jnp.where(left, _dot(wbs[0], xsb), _dot(wbs[1], xsb))
            dec = jnp.where(left, jnp.broadcast_to(etot[:, 2 * pair:2 * pair + 1], (1, LANES)),
                            jnp.broadcast_to(etot[:, 2 * pair + 1:2 * pair + 2], (1, LANES)))
            ht_scr[pair] = ht * dec + upd
        return carry

    lax.fori_loop(0, t // q, chunk, 0)

    @pl.when(c == pl.num_programs(1) - 1)
    def _():
        for pair in range(N_PAIRS):
            ssm_ref[pair * LANES:(pair + 1) * LANES, :] = ht_scr[pair].T

    gz = y_scr[...] * _silu(_dot(ub, wsz_ref[...]))
    parts = []
    for g in range(SSM_GROUPS):
        gg = gz[:, g * GROUP_W:(g + 1) * GROUP_W]
        parts.append(gg * lax.rsqrt(jnp.mean(gg * gg, axis=-1, keepdims=True) + EPS))
    gn = jnp.concatenate(parts, axis=1) * snw_ref[...]
    y_s = _dot(gn.astype(BF16), wso_ref[...])
    merged = acc_ref[...] + _sigmoid(_dot(ub, wgs_ref[...])) * y_s
    xn = x + _dot(merged.astype(BF16), wo_ref[...])
    out_ref[...] = _rms(xn, fnw_ref[...]) if final_norm else xn


def _prompt_ssd(x, acc, nw, wsz, wsx, wdt, wgs, csw, csb, dtb, alog, dsk, snw, wso, wo, fnw, tile, final_norm):
    bt, seq, _ = x.shape
    nc = seq // tile
    tok = lambda w: pl.BlockSpec((None, tile, w), lambda b, c: (b, c, 0))
    per_b = lambda r, w: pl.BlockSpec((None, r, w), lambda b, c: (b, 0, 0))
    return pl.pallas_call(
        functools.partial(_ssd_kernel, final_norm=final_norm),
        grid=(bt, nc),
        in_specs=[tok(D_MODEL), tok(D_MODEL), _const((1, D_MODEL)), _const((D_MODEL, SSM_INNER)),
                  _const((D_MODEL, SSM_XBC)), _const((D_MODEL, LANES)), _const((D_MODEL, D_MODEL)),
                  _const((4, SSM_XBC)), _const((1, SSM_XBC)), _const((1, LANES)), _const((1, LANES)),
                  _const((1, SSM_INNER)), _const((1, SSM_INNER)), _const((SSM_INNER, D_MODEL)),
                  _const((D_MODEL, D_MODEL)), _const((1, D_MODEL))],
        out_specs=[tok(D_MODEL), per_b(3, SSM_XBC), per_b(SSM_HEADS * SSM_HEADDIM, SSM_STATE)],
        out_shape=[jax.ShapeDtypeStruct((bt, seq, D_MODEL), F32),
                   jax.ShapeDtypeStruct((bt, 3, SSM_XBC), F32),
                   jax.ShapeDtypeStruct((bt, SSM_HEADS * SSM_HEADDIM, SSM_STATE), F32)],
        scratch_shapes=[pltpu.VMEM((tile + SUBLANES, SSM_XBC), F32),
                        pltpu.VMEM((tile, SSM_XBC), F32),
                        pltpu.VMEM((tile, LANES), F32),
                        pltpu.VMEM((tile, SSM_INNER), F32),
                        pltpu.VMEM((N_PAIRS, SSM_STATE, LANES), F32)],
        compiler_params=_params(("arbitrary", "arbitrary")),
        name="prompt_ssd",
    )(x, acc, nw, wsz, wsx, wdt, wgs, csw, csb, dtb, alog, dsk, snw, wso, wo, fnw)


def _proj_kernel(x_ref, nw_ref, w_ref, o_ref):
    o_ref[...] = _dot(_rms(x_ref[...], nw_ref[...]).astype(BF16), w_ref[...])


def _sample_proj(x, nw, w_all, tn):
    m = x.shape[0]
    n = w_all.shape[1]
    return pl.pallas_call(
        _proj_kernel,
        grid=(n // tn,),
        in_specs=[_const((m, D_MODEL)), _const((1, D_MODEL)), pl.BlockSpec((D_MODEL, tn), lambda j: (0, j))],
        out_specs=pl.BlockSpec((m, tn), lambda j: (0, j)),
        out_shape=jax.ShapeDtypeStruct((m, n), F32),
        compiler_params=_params(("arbitrary",)),
        name="sample_proj",
    )(x, nw, w_all)


_S_A = 0
_S_SZ = 4 * CONV_W
_S_SX = _S_SZ + SSM_INNER
_S_DT = _S_SX + SSM_XBC
_S_M = _S_DT + SSM_INNER
_S_G = _S_M + 2 * ATTN_W
_S_END = _S_G + 3 * D_MODEL
_R_DA = SSM_INNER
_R_B = 2 * SSM_INNER
_R_C = _R_B + SSM_GROUPS * SSM_STATE
_R_END = _R_C + SSM_GROUPS * SSM_STATE


def _pre_kernel(p_ref, sca_ref, scs_ref, caw_ref, csw_ref, csb_ref, dtb_ref, alog_ref,
                acta_ref, row_ref, xs_ref, nca_ref, ncs_ref):
    cb = p_ref[:, _S_A:_S_A + CONV_W]
    v = p_ref[:, _S_A + CONV_W:_S_A + 2 * CONV_W] * p_ref[:, _S_A + 2 * CONV_W:_S_A + 3 * CONV_W]
    cz = p_ref[:, _S_A + 3 * CONV_W:_S_A + 4 * CONV_W]
    caw = caw_ref[...]
    b0 = sca_ref[:, 0:CONV_W]
    b1 = sca_ref[:, CONV_W:2 * CONV_W]
    conv = caw[0:1] * b0 + caw[1:2] * b1 + caw[2:3] * v
    acta_ref[...] = _silu(cz) * cb * conv
    nca_ref[:, 0:CONV_W] = b1
    nca_ref[:, CONV_W:2 * CONV_W] = v

    sxbc = p_ref[:, _S_SX:_S_SX + SSM_XBC]
    csw = csw_ref[...]
    s0 = scs_ref[:, 0:SSM_XBC]
    s1 = scs_ref[:, SSM_XBC:2 * SSM_XBC]
    s2 = scs_ref[:, 2 * SSM_XBC:3 * SSM_XBC]
    xbc = _silu(csw[0:1] * s0 + csw[1:2] * s1 + csw[2:3] * s2 + csw[3:4] * sxbc + csb_ref[...])
    ncs_ref[:, 0:SSM_XBC] = s1
    ncs_ref[:, SSM_XBC:2 * SSM_XBC] = s2
    ncs_ref[:, 2 * SSM_XBC:3 * SSM_XBC] = sxbc
    xs = xbc[:, :SSM_INNER]
    dt = _softplus(p_ref[:, _S_DT:_S_DT + SSM_INNER] + dtb_ref[...])
    xs_ref[...] = xs
    row_ref[:, 0:_R_DA] = xs * dt
    row_ref[:, _R_DA:_R_B] = jnp.exp(dt * -jnp.exp(alog_ref[...]))
    row_ref[:, _R_B:_R_END] = xbc[:, SSM_INNER:]


def _sample_pre(proj, sca, scs, caw, csw, csb, dtb_e, alog_e, rb):
    m = proj.shape[0]
    rows = lambda w: pl.BlockSpec((rb, w), lambda i: (i, 0))
    return pl.pallas_call(
        _pre_kernel,
        grid=(m // rb,),
        in_specs=[rows(_S_END), rows(2 * CONV_W), rows(3 * SSM_XBC), _const((3, CONV_W)), _const((4, SSM_XBC)),
                  _const((1, SSM_XBC)), _const((1, SSM_INNER)), _const((1, SSM_INNER))],
        out_specs=[rows(CONV_W), rows(_R_END), rows(SSM_INNER), rows(2 * CONV_W), rows(3 * SSM_XBC)],
        out_shape=[jax.ShapeDtypeStruct((m, CONV_W), F32),
                   jax.ShapeDtypeStruct((m, _R_END), F32),
                   jax.ShapeDtypeStruct((m, SSM_INNER), F32),
                   jax.ShapeDtypeStruct((m, 2 * CONV_W), F32),
                   jax.ShapeDtypeStruct((m, 3 * SSM_XBC), F32)],
        compiler_params=_params(("arbitrary",)),
        name="sample_pre",
    )(proj, sca, scs, caw, csw, csb, dtb_e, alog_e)


def _sssd_kernel(row_ref, h0_ref, *rest):
    hn_ref, y_ref = rest[-2:]
    nb = row_ref.shape[0]
    sub_x = lax.broadcasted_iota(jnp.int32, (LANES, GROUP_W), 0)
    sub_r = lax.broadcasted_iota(jnp.int32, (LANES, 2 * SSM_STATE), 0)
    sub_c = lax.broadcasted_iota(jnp.int32, (SUBLANES, SSM_STATE), 0)
    for i in range(nb):
        row = row_ref[i]
        ys = []
        for g in range(SSM_GROUPS):
            ch = slice(g * GROUP_W, (g + 1) * GROUP_W)
            xdt = row[:, ch]
            x_hi = xdt.astype(BF16).astype(F32)
            x_lo = xdt - x_hi
            da = row[:, _R_DA + g * GROUP_W:_R_DA + (g + 1) * GROUP_W]
            a_hi = da.astype(BF16).astype(F32)
            a_mid = (da - a_hi).astype(BF16).astype(F32)
            a_lo = da - a_hi - a_mid
            xk = jnp.where((sub_x == 0) | (sub_x == 2), x_hi,
                           jnp.where((sub_x == 1) | (sub_x == 3), x_lo,
                                     jnp.where(sub_x == 4, a_hi,
                                               jnp.where(sub_x == 5, a_mid,
                                                         jnp.where(sub_x == 6, a_lo, 0.0)))))
            bg = row[:, _R_B + g * SSM_STATE:_R_B + (g + 1) * SSM_STATE]
            b_hi = bg.astype(BF16).astype(F32)
            b_lo = bg - b_hi
            zero = jnp.zeros((1, SSM_STATE), F32)
            rk = jnp.where(sub_r < 2, jnp.concatenate([b_hi, zero], axis=1),
                           jnp.where(sub_r < 4, jnp.concatenate([b_lo, zero], axis=1),
                                     jnp.where(sub_r < 7, jnp.concatenate([zero, zero + 1.0], axis=1), 0.0)))
            bc = _dot(xk.T.astype(BF16), rk.astype(BF16))
            hnew = bc[:, SSM_STATE:] * h0_ref[i, ch, :] + bc[:, :SSM_STATE]
            hn_ref[i, ch, :] = hnew
            cg = row[:, _R_C + g * SSM_STATE:_R_C + (g + 1) * SSM_STATE]
            c8 = jnp.where(sub_c == 0, cg, 0.0).astype(BF16)
            ys.append(_dot_nt(c8, hnew.astype(BF16))[0:1])
        y_ref[i] = jnp.concatenate(ys, axis=1)


def _sample_ssd(rows, state_all, layer, stacked, nb):
    depth, m, hp, _ = state_all.shape
    state_spec = pl.BlockSpec((None, nb, hp, SSM_STATE), lambda i: (layer, i, 0, 0))
    in_specs = [pl.BlockSpec((nb, 1, _R_END), lambda i: (i, 0, 0)), state_spec]
    args = [rows.reshape(m, 1, _R_END), state_all]
    aliases = {}
    if stacked is not None:
        in_specs.append(pl.BlockSpec(memory_space=pl.ANY))
        args.append(stacked)
        aliases = {2: 0}
    return pl.pallas_call(
        _sssd_kernel,
        grid=(m // nb,),
        in_specs=in_specs,
        out_specs=[state_spec, pl.BlockSpec((nb, 1, hp), lambda i: (i, 0, 0))],
        out_shape=[jax.ShapeDtypeStruct((depth, m, hp, SSM_STATE), F32),
                   jax.ShapeDtypeStruct((m, 1, hp), F32)],
        input_output_aliases=aliases,
        compiler_params=_params(("arbitrary",)),
        name="sample_ssd",
    )(*args)


def _sattn_kernel(q_ref, k_ref, v_ref, o_ref):
    nb = q_ref.shape[0]
    halves = MEM_HEADDIM // LANES
    rows = halves * MEM_HEADS
    row = lax.broadcasted_iota(jnp.int32, (rows, CACHE_ROWS), 0)
    col = lax.broadcasted_iota(jnp.int32, (rows, CACHE_ROWS), 1)
    own = (col & (rows - 1)) == row
    lane = lax.broadcasted_iota(jnp.int32, (1, LANES), 1)
    low_half = (lane & MEM_HEADS) == 0
    for i in range(nb):
        q = q_ref[i]
        q8 = jnp.concatenate([q[:, (h * halves + j) * LANES:(h * halves + j + 1) * LANES]
                              for j in range(halves) for h in range(MEM_HEADS)], axis=0)
        s = _dot_nt(q8.astype(BF16), k_ref[i].astype(BF16))
        t = jnp.sum(jnp.where(own, s, 0.0), axis=0, keepdims=True)
        parts = []
        for b in range(CACHE_ROWS // LANES):
            tb = t[:, b * LANES:(b + 1) * LANES]
            parts.append(tb + jnp.where(low_half, pltpu.roll(tb, LANES - MEM_HEADS, 1),
                                        pltpu.roll(tb, MEM_HEADS, 1)))
        u = jnp.concatenate(parts, axis=1) * (MEM_HEADDIM ** -0.5)
        p = _softmax_rows(jnp.where(own, u, -jnp.inf))
        o8 = _dot(p.astype(BF16), v_ref[i].astype(BF16))
        for j in range(halves):
            for h in range(MEM_HEADS):
                r = j * MEM_HEADS + h
                o_ref[i, :, (h * halves + j) * LANES:(h * halves + j + 1) * LANES] = o8[r:r + 1, :]


def _cache_rows(c):
    depth, m = c.shape[:2]
    c = c.reshape(depth, m, N_MEM, MEM_HEADS, MEM_HEADDIM // LANES, LANES)
    return c.transpose(0, 1, 2, 4, 3, 5).reshape(depth, m, CACHE_ROWS, LANES)


def _sample_attn(q, k_rows, v_rows, layer, nb):
    m = q.shape[0]
    cache = pl.BlockSpec((None, nb, CACHE_ROWS, LANES), lambda i: (layer, i, 0, 0))
    return pl.pallas_call(
        _sattn_kernel,
        grid=(m // nb,),
        in_specs=[pl.BlockSpec((nb, 1, ATTN_W), lambda i: (i, 0, 0)), cache, cache],
        out_specs=pl.BlockSpec((nb, 1, ATTN_W), lambda i: (i, 0, 0)),
        out_shape=jax.ShapeDtypeStruct((m, 1, ATTN_W), F32),
        compiler_params=_params(("arbitrary",)),
        name="sample_attn",
    )(q.reshape(m, 1, ATTN_W), k_rows, v_rows)


def _post_kernel(x_ref, p_ref, acta_ref, xs_ref, ysd_ref, o_ref, dsk_ref, snw_ref, wao_ref, wso_ref, wmo_ref,
                 wo_ref, fnw_ref, out_ref, *, final_norm):
    y_a = _dot(acta_ref[...].astype(BF16), wao_ref[...])
    gz = (ysd_ref[...] + dsk_ref[...] * xs_ref[...]) * _silu(p_ref[:, _S_SZ:_S_SZ + SSM_INNER])
    parts = []
    for g in range(SSM_GROUPS):
        gg = gz[:, g * GROUP_W:(g + 1) * GROUP_W]
        parts.append(gg * lax.rsqrt(jnp.mean(gg * gg, axis=-1, keepdims=True) + EPS))
    gn = jnp.concatenate(parts, axis=1) * snw_ref[...]
    y_s = _dot(gn.astype(BF16), wso_ref[...])
    mz = p_ref[:, _S_M + ATTN_W:_S_M + 2 * ATTN_W]
    y_m = _dot((_silu(mz) * o_ref[...]).astype(BF16), wmo_ref[...])
    merged = (_sigmoid(p_ref[:, _S_G:_S_G + D_MODEL]) * y_a
              + _sigmoid(p_ref[:, _S_G + D_MODEL:_S_G + 2 * D_MODEL]) * y_s
              + _sigmoid(p_ref[:, _S_G + 2 * D_MODEL:_S_G + 3 * D_MODEL]) * y_m)
    xn = x_ref[...] + _dot(merged.astype(BF16), wo_ref[...])
    out_ref[...] = _rms(xn, fnw_ref[...]) if final_norm else xn


def _sample_post(x, proj, acta, xs, ysd, o, dsk_e, snw, wao, wso, wmo, wo, fnw, final_norm):
    m = x.shape[0]
    full = lambda a: _const(a.shape)
    args = (x, proj, acta, xs, ysd, o, dsk_e, snw, wao, wso, wmo, wo, fnw)
    return pl.pallas_call(
        functools.partial(_post_kernel, final_norm=final_norm),
        grid=(1,),
        in_specs=[full(a) for a in args],
        out_specs=_const((m, D_MODEL)),
        out_shape=jax.ShapeDtypeStruct((m, D_MODEL), F32),
        compiler_params=_params(("arbitrary",)),
        name="sample_post",
    )(*args)


def _layer_weights(l, norm_w, w_in, conv_a_w, w_a_out, conv_s_w, conv_s_b, dt_bias, a_log, d_skip, ssm_norm_w,
                   w_s_out, mem_norm_w, w_mem_kv, w_m_out, w_o):
    wb = w_in[l].astype(BF16)
    w = {}
    w["nw"] = norm_w[l][None]
    w["wa"] = wb[:, _OFF_A:_OFF_SZ]
    w["wsz"] = wb[:, _OFF_SZ:_OFF_SX]
    w["wsx"] = wb[:, _OFF_SX:_OFF_DT]
    wdt = wb[:, _OFF_DT:_OFF_M]
    w["wdt"] = jnp.pad(wdt, ((0, 0), (0, LANES - SSM_HEADS)))
    w["wdt_e"] = jnp.repeat(wdt, SSM_HEADDIM, axis=1)
    w["wm"] = wb[:, _OFF_M:_OFF_G]
    wg = wb[:, _OFF_G:]
    w["wg"] = wg
    w["wg_am"] = jnp.concatenate([wg[:, :D_MODEL], wg[:, 2 * D_MODEL:]], axis=1)
    w["wg_s"] = wg[:, D_MODEL:2 * D_MODEL]
    w["caw"] = conv_a_w[l]
    w["wao"] = w_a_out[l].astype(BF16)
    w["csw"] = conv_s_w[l]
    w["csb"] = conv_s_b[l][None]
    w["dtb"] = jnp.pad(dt_bias[l], (0, LANES - SSM_HEADS))[None]
    w["alog"] = jnp.pad(a_log[l], (0, LANES - SSM_HEADS))[None]
    w["dtb_e"] = jnp.repeat(dt_bias[l], SSM_HEADDIM)[None]
    w["alog_e"] = jnp.repeat(a_log[l], SSM_HEADDIM)[None]
    w["dsk_e"] = jnp.repeat(d_skip[l], SSM_HEADDIM)[None]
    w["snw"] = ssm_norm_w[l][None]
    w["wso"] = w_s_out[l].astype(BF16)
    w["mnw"] = mem_norm_w[l][None]
    w["wkv"] = w_mem_kv[l].astype(BF16)
    w["wmo"] = w_m_out[l].astype(BF16)
    w["wo"] = w_o[l].astype(BF16)
    w["w_all"] = jnp.concatenate([w["wa"], w["wsz"], w["wsx"], w["wdt_e"], w["wm"], wg], axis=1)
    return w


def _prompt_tile(seq):
    for t in (256, 128):
        if seq % t == 0:
            return t
    raise ValueError(f"sequence length {seq} must be a multiple of {SSD_CHUNK}")


def kernel(x_prompt, x_sample, mem_prompt, state_conv_a, state_conv_s, state_ssm, cache_mem_k, cache_mem_v,
           norm_w, w_in, conv_a_w, w_a_out, conv_s_w, conv_s_b, dt_bias, a_log, d_skip, ssm_norm_w,
           w_s_out, mem_norm_w, w_mem_kv, w_m_out, w_o, final_norm_w):
    depth = w_in.shape[0]
    bp, seq, _ = x_prompt.shape
    bs = x_sample.shape[0]
    assert x_sample.shape[1] == 1
    fnw = final_norm_w[None]
    tile = _prompt_tile(seq)
    weights = [_layer_weights(l, norm_w, w_in, conv_a_w, w_a_out, conv_s_w, conv_s_b, dt_bias, a_log, d_skip,
                              ssm_norm_w, w_s_out, mem_norm_w, w_mem_kv, w_m_out, w_o) for l in range(depth)]

    x = x_prompt
    pa, ps, ph, pk, pv = [], [], [], [], []
    for l, w in enumerate(weights):
        last = l == depth - 1
        k, v, kt, vb = _mem_kv(mem_prompt, w["mnw"], w["wkv"])
        acc, ca = _prompt_am(x, w["nw"], w["wa"], w["wm"], w["wg_am"], w["caw"], w["wao"], w["wmo"], kt, vb, tile)
        x, cs, hs = _prompt_ssd(x, acc, w["nw"], w["wsz"], w["wsx"], w["wdt"], w["wg_s"], w["csw"], w["csb"],
                                w["dtb"], w["alog"], w["dsk_e"], w["snw"], w["wso"], w["wo"], fnw, tile, last)
        pa.append(ca)
        ps.append(cs)
        ph.append(hs.reshape(bp, SSM_HEADS, SSM_HEADDIM, SSM_STATE))
        pk.append(k.reshape(bp, N_MEM, MEM_HEADS, MEM_HEADDIM))
        pv.append(v.reshape(bp, N_MEM, MEM_HEADS, MEM_HEADDIM))
    y_prompt = x

    x = x_sample.reshape(bs, D_MODEL)
    state_all = state_ssm.reshape(depth, bs, SSM_HEADS * SSM_HEADDIM, SSM_STATE)
    k_rows, v_rows = _cache_rows(cache_mem_k), _cache_rows(cache_mem_v)
    sa, ss, s_ssm = [], [], None
    for l, w in enumerate(weights):
        last = l == depth - 1
        proj = _sample_proj(x, w["nw"], w["w_all"], 2048)
        acta, rows, xs, nca, ncs = _sample_pre(
            proj, state_conv_a[l].reshape(bs, 2 * CONV_W), state_conv_s[l].reshape(bs, 3 * SSM_XBC),
            w["caw"], w["csw"], w["csb"], w["dtb_e"], w["alog_e"], 32)
        s_ssm, ysd = _sample_ssd(rows, state_all, l, s_ssm, 2)
        o = _sample_attn(proj[:, _S_M:_S_M + ATTN_W], k_rows, v_rows, l, 2)
        x = _sample_post(x, proj, acta, xs, ysd.reshape(bs, SSM_INNER), o.reshape(bs, ATTN_W), w["dsk_e"],
                         w["snw"], w["wao"], w["wso"], w["wmo"], w["wo"], fnw, last)
        sa.append(nca.reshape(bs, 2, CONV_W))
        ss.append(ncs.reshape(bs, 3, SSM_XBC))
    y_sample = x.reshape(bs, 1, D_MODEL)

    return (y_prompt, y_sample, jnp.stack(pa), jnp.stack(ps), jnp.stack(ph), jnp.stack(pk), jnp.stack(pv),
            jnp.stack(sa), jnp.stack(ss), s_ssm.reshape(depth, bs, SSM_HEADS, SSM_HEADDIM, SSM_STATE))
```

```python
import functools

import jax
import jax.numpy as jnp
from jax import lax
from jax.experimental import pallas as pl
from jax.experimental.pallas import tpu as pltpu

F32 = jnp.float32
BF16 = jnp.bfloat16

D_MODEL = 1024
CONV_W = 1024
SSM_INNER = 2048
SSM_HEADDIM = 64
SSM_HEADS = 32
SSM_GROUPS = 4
SSM_STATE = 128
SSM_XBC = SSM_INNER + 2 * SSM_GROUPS * SSM_STATE
HEADS_PER_GROUP = SSM_HEADS // SSM_GROUPS
GROUP_W = SSM_INNER // SSM_GROUPS
N_PAIRS = SSM_HEADS // 2
N_MEM = 256
MEM_HEADS = 4
MEM_HEADDIM = 256
ATTN_W = MEM_HEADS * MEM_HEADDIM
EPS = 1e-6
LOG2E = 1.4426950408889634
SSD_CHUNK = 128
LANES = 128
SUBLANES = 8
CACHE_ROWS = N_MEM * ATTN_W // LANES
VMEM_LIMIT = 58 * 1024 * 1024
WBLK = 1024
STRIPS = WBLK // LANES

_B_CB, _B_CC, _B_CH, _B_CZ = 0, 1, 2, 3
_B_SZ = 4
_B_SX = 6
_B_DT = 9
_B_Q, _B_MZ = 11, 12
_B_GA, _B_GS, _B_GM = 13, 14, 15
_N_BLK = 16
_S_A = 0
_S_SZ = _B_SZ * WBLK
_S_SX = _B_SX * WBLK
_S_DT = _B_DT * WBLK
_S_M = _B_Q * WBLK
_S_G = _B_GA * WBLK
_S_END = _N_BLK * WBLK
_OFF_DT = 4 * CONV_W + SSM_INNER + SSM_XBC
_OFF_M = _OFF_DT + SSM_HEADS
_R_DA = SSM_INNER
_R_B = 2 * SSM_INNER
_R_C = _R_B + SSM_GROUPS * SSM_STATE
_R_END = _R_C + SSM_GROUPS * SSM_STATE

_V_NW = 0
_V_MNW = _V_NW + D_MODEL
_V_CSB = _V_MNW + D_MODEL
_V_DTB = _V_CSB + SSM_XBC
_V_ALOG = _V_DTB + LANES
_V_DSK = _V_ALOG + LANES
_V_SNW = _V_DSK + SSM_INNER
_V_DTBE = _V_SNW + SSM_INNER
_V_ALOGE = _V_DTBE + SSM_INNER
_V_FNW = _V_ALOGE + SSM_INNER
_V_END = _V_FNW + D_MODEL


def _dot(a, b):
    return jnp.dot(a, b, preferred_element_type=F32)


def _dot_nt(a, b):
    return lax.dot_general(a, b, (((1,), (1,)), ((), ())), preferred_element_type=F32)


def _rms(x, w):
    return x * lax.rsqrt(jnp.mean(x * x, axis=-1, keepdims=True) + EPS) * w


def _sigmoid(x):
    return 0.5 * jnp.tanh(0.5 * x) + 0.5


def _silu(x):
    h = 0.5 * x
    return h * jnp.tanh(h) + h


def _softplus(x):
    return jnp.maximum(x, 0.0) + jnp.log1p(jnp.exp(-jnp.abs(x)))


def _softmax_rows(s):
    m = jnp.max(s, axis=-1, keepdims=True)
    e = jnp.exp(s - m)
    return e / jnp.sum(e, axis=-1, keepdims=True)


def _split3(x):
    hi = x.astype(BF16)
    r1 = x - hi.astype(F32)
    mid = r1.astype(BF16)
    lo = (r1 - mid.astype(F32)).astype(BF16)
    return hi, mid, lo


def _vec(vec_ref, off, width):
    return vec_ref[:, off:off + width]


def _resident(block, index):
    return pl.BlockSpec(block, lambda *_: index, pipeline_mode=pl.Buffered(1))


def _wblk(layer, blk):
    return _resident((None, D_MODEL, WBLK), (layer, 0, blk))


def _vec_spec(layer):
    return _resident((None, 1, _V_END), (layer, 0, 0))


def _layer_mat(shape, layer):
    return _resident((None,) + shape, (layer, 0, 0))


def _params(sem):
    return pltpu.CompilerParams(dimension_semantics=sem, vmem_limit_bytes=VMEM_LIMIT)


def _kv_kernel(mem_ref, vec_ref, w_ref, k_ref, v_ref, kt_ref, vb_ref):
    u = _rms(mem_ref[...], _vec(vec_ref, _V_MNW, D_MODEL)).astype(BF16)
    kv = _dot(u, w_ref[...])
    k = kv[:, :ATTN_W]
    v = kv[:, ATTN_W:]
    k_ref[...] = k
    v_ref[...] = v
    kt_ref[...] = (k.T * (MEM_HEADDIM ** -0.5)).astype(BF16)
    vb_ref[...] = v.astype(BF16)


def _mem_kv(mem, vec, w_kv, layer):
    bt = mem.shape[0]
    blk = lambda r, c: pl.BlockSpec((None, r, c), lambda b: (b, 0, 0))
    return pl.pallas_call(
        _kv_kernel,
        grid=(bt,),
        in_specs=[blk(N_MEM, D_MODEL), _vec_spec(layer), _layer_mat((D_MODEL, 2 * ATTN_W), layer)],
        out_specs=[blk(N_MEM, ATTN_W), blk(N_MEM, ATTN_W), blk(ATTN_W, N_MEM), blk(N_MEM, ATTN_W)],
        out_shape=[jax.ShapeDtypeStruct((bt, N_MEM, ATTN_W), F32),
                   jax.ShapeDtypeStruct((bt, N_MEM, ATTN_W), F32),
                   jax.ShapeDtypeStruct((bt, ATTN_W, N_MEM), BF16),
                   jax.ShapeDtypeStruct((bt, N_MEM, ATTN_W), BF16)],
        compiler_params=_params(("arbitrary",)),
        name="prompt_kv",
    )(mem, vec, w_kv)


def _am_kernel(x_ref, vec_ref, wcb_ref, wcc_ref, wch_ref, wcz_ref, wq_ref, wmz_ref, wga_ref, wgm_ref,
               caw_ref, wao_ref, wmo_ref, kt_ref, vb_ref,
               acc_ref, ca_ref, vbuf):
    c = pl.program_id(1)
    t = x_ref.shape[0]
    pad = SUBLANES

    @pl.when(c == 0)
    def _():
        vbuf[:, 0:pad, :] = jnp.zeros((STRIPS, pad, LANES), F32)

    ub = _rms(x_ref[...], _vec(vec_ref, _V_NW, D_MODEL)).astype(BF16)

    v = _dot(ub, wcc_ref[...]) * _dot(ub, wch_ref[...])
    gate = _silu(_dot(ub, wcz_ref[...])) * _dot(ub, wcb_ref[...])
    caw = caw_ref[...]
    acts = []
    for j in range(STRIPS):
        sl = slice(j * LANES, (j + 1) * LANES)
        vj = v[:, sl]
        vbuf[j, pad:pad + t, :] = vj
        conv = (caw[0:1, sl] * vbuf[j, pad - 2:pad - 2 + t, :] + caw[1:2, sl] * vbuf[j, pad - 1:pad - 1 + t, :]
                + caw[2:3, sl] * vj)
        acts.append((gate[:, sl] * conv).astype(BF16))
        ca_ref[:, sl] = vbuf[j, pad + t - 2:pad + t, :]
        vbuf[j, 0:pad, :] = vbuf[j, t:t + pad, :]
    y_a = _dot(jnp.concatenate(acts, axis=1), wao_ref[...])

    q = _dot(ub, wq_ref[...])
    mz = _dot(ub, wmz_ref[...])
    acts = []
    for h in range(MEM_HEADS):
        sl = slice(h * MEM_HEADDIM, (h + 1) * MEM_HEADDIM)
        p = _softmax_rows(_dot(q[:, sl].astype(BF16), kt_ref[sl, :]))
        o = _dot(p.astype(BF16), vb_ref[:, sl])
        acts.append((_silu(mz[:, sl]) * o).astype(BF16))
    y_m = _dot(jnp.concatenate(acts, axis=1), wmo_ref[...])

    acc_ref[...] = _sigmoid(_dot(ub, wga_ref[...])) * y_a + _sigmoid(_dot(ub, wgm_ref[...])) * y_m


def _prompt_am(x, vec, w_all, caw, wao, wmo, kt, vb, layer, tile):
    bt, seq, _ = x.shape
    nc = seq // tile
    tok = lambda w: pl.BlockSpec((None, tile, w), lambda b, c: (b, c, 0))
    per_b = lambda r, w: pl.BlockSpec((None, r, w), lambda b, c: (b, 0, 0))
    blocks = (_B_CB, _B_CC, _B_CH, _B_CZ, _B_Q, _B_MZ, _B_GA, _B_GM)
    return pl.pallas_call(
        _am_kernel,
        grid=(bt, nc),
        in_specs=([tok(D_MODEL), _vec_spec(layer)] + [_wblk(layer, b) for b in blocks]
                  + [_layer_mat((3, CONV_W), layer), _layer_mat((CONV_W, D_MODEL), layer),
                     _layer_mat((ATTN_W, D_MODEL), layer), per_b(ATTN_W, N_MEM), per_b(N_MEM, ATTN_W)]),
        out_specs=[tok(D_MODEL), per_b(2, CONV_W)],
        out_shape=[jax.ShapeDtypeStruct((bt, seq, D_MODEL), F32),
                   jax.ShapeDtypeStruct((bt, 2, CONV_W), F32)],
        scratch_shapes=[pltpu.VMEM((STRIPS, tile + SUBLANES, LANES), F32)],
        compiler_params=_params(("arbitrary", "arbitrary")),
        name="prompt_am",
    )(x, vec, *([w_all] * len(blocks)), caw, wao, wmo, kt, vb)


def _cumsum_rows(da):
    q = da.shape[0]
    r = lax.broadcasted_iota(jnp.int32, (q, q), 0)
    c = lax.broadcasted_iota(jnp.int32, (q, q), 1)
    tri = jnp.where(r >= c, 1.0, 0.0).astype(BF16)
    hi, mid, lo = _split3(da)
    return _dot(tri, hi) + _dot(tri, mid) + _dot(tri, lo)


def _ssd_kernel(x_ref, acc_ref, vec_ref, wsz0_ref, wsz1_ref, wsx0_ref, wsx1_ref, wsx2_ref, wgs_ref, wdt_ref,
                csw_ref, wso_ref, wo_ref,
                out_ref, cs_ref, ssm_ref,
                xbuf, tails, xbc_scr, dt_scr, ht_scr, *, final_norm):
    c = pl.program_id(1)
    t = x_ref.shape[0]
    q = SSD_CHUNK
    pad = SUBLANES

    @pl.when(c == 0)
    def _():
        tails[...] = jnp.zeros(tails.shape, F32)
        ht_scr[...] = jnp.zeros(ht_scr.shape, F32)

    ub = _rms(x_ref[...], _vec(vec_ref, _V_NW, D_MODEL)).astype(BF16)

    csw = csw_ref[...]
    for b, w_ref in enumerate((wsx0_ref, wsx1_ref, wsx2_ref)):
        sx = _dot(ub, w_ref[...])
        for j in range(STRIPS):
            s = b * STRIPS + j
            sl = slice(s * LANES, (s + 1) * LANES)
            sxj = sx[:, j * LANES:(j + 1) * LANES]
            xbuf[j, 0:pad, :] = tails[s]
            xbuf[j, pad:pad + t, :] = sxj
            yc = (csw[0:1, sl] * xbuf[j, pad - 3:pad - 3 + t, :] + csw[1:2, sl] * xbuf[j, pad - 2:pad - 2 + t, :]
                  + csw[2:3, sl] * xbuf[j, pad - 1:pad - 1 + t, :] + csw[3:4, sl] * sxj
                  + vec_ref[:, _V_CSB + s * LANES:_V_CSB + (s + 1) * LANES])
            xbc_scr[:, sl] = _silu(yc)
            cs_ref[:, sl] = xbuf[j, pad + t - 3:pad + t, :]
            tails[s] = xbuf[j, t:t + pad, :]
    dt_scr[...] = _softplus(_dot(ub, wdt_ref[...]) + _vec(vec_ref, _V_DTB, LANES))
    a_row = -jnp.exp(_vec(vec_ref, _V_ALOG, LANES))

    rr = lax.broadcasted_iota(jnp.int32, (q, q), 0)
    cc = lax.broadcasted_iota(jnp.int32, (q, q), 1)
    tril = rr >= cc
    left = lax.broadcasted_iota(jnp.int32, (1, LANES), 1) < SSM_HEADDIM

    def chunk(j):
        rows = slice(j * q, (j + 1) * q)
        dtj = dt_scr[rows, :]
        cs = _cumsum_rows(dtj * a_row) * LOG2E
        cs_t = cs.T
        dt_t = dtj.T
        w_t = dt_t * jnp.exp2(cs_t[:, q - 1:q] - cs_t)
        etot = jnp.exp2(cs[q - 1:q, :])

        scores, bm_t, cms = [], [], []
        for g in range(SSM_GROUPS):
            bg = xbc_scr[rows, SSM_INNER + g * SSM_STATE:SSM_INNER + (g + 1) * SSM_STATE]
            cg = xbc_scr[rows, SSM_INNER + (SSM_GROUPS + g) * SSM_STATE:
                         SSM_INNER + (SSM_GROUPS + g + 1) * SSM_STATE]
            scores.append(_dot_nt(cg.astype(BF16), bg.astype(BF16)))
            bm_t.append(bg.T)
            cms.append(cg)

        for pair in range(N_PAIRS):
            g = (2 * pair) // HEADS_PER_GROUP
            lanes = slice(pair * LANES, (pair + 1) * LANES)
            xs_p = xbc_scr[rows, lanes]
            ht = ht_scr[pair]
            xsb = xs_p.astype(BF16)
            rhs = jnp.concatenate([xsb, ht.astype(BF16)], axis=0)
            ys, upds = [], []
            for h in (2 * pair, 2 * pair + 1):
                col = jnp.broadcast_to(cs[:, h:h + 1], (q, q))
                row = jnp.broadcast_to(cs_t[h:h + 1, :], (q, q))
                lmat = jnp.exp2(jnp.where(tril, col - row, -jnp.inf))
                lhs = jnp.concatenate([(scores[g] * lmat * dt_t[h:h + 1, :]).astype(BF16),
                                       (cms[g] * jnp.exp2(col)).astype(BF16)], axis=1)
                ys.append(_dot(lhs, rhs))
                upds.append(_dot((bm_t[g] * w_t[h:h + 1, :]).astype(BF16), xsb))
            xbc_scr[rows, lanes] = (jnp.where(left, ys[0], ys[1])
                                    + vec_ref[:, _V_DSK + pair * LANES:_V_DSK + (pair + 1) * LANES] * xs_p)
            dec = jnp.where(left, jnp.broadcast_to(etot[:, 2 * pair:2 * pair + 1], (1, LANES)),
                            jnp.broadcast_to(etot[:, 2 * pair + 1:2 * pair + 2], (1, LANES)))
            ht_scr[pair] = ht * dec + jnp.where(left, upds[0], upds[1])

    side = {0: lambda: _silu(_dot(ub, wsz0_ref[...])), 1: lambda: _silu(_dot(ub, wsz1_ref[...])),
            2: lambda: _sigmoid(_dot(ub, wgs_ref[...]))}
    side_vals = {}
    for j in range(t // q):
        chunk(j)
        if j in side:
            side_vals[j] = side[j]()
    for j in side:
        if j not in side_vals:
            side_vals[j] = side[j]()
    zgate = (side_vals[0], side_vals[1])
    g_s = side_vals[2]

    @pl.when(c == pl.num_programs(1) - 1)
    def _():
        for pair in range(N_PAIRS):
            ssm_ref[pair * LANES:(pair + 1) * LANES, :] = ht_scr[pair].T

    parts = []
    for b in range(SSM_INNER // WBLK):
        gz = xbc_scr[:, b * WBLK:(b + 1) * WBLK] * zgate[b]
        for g in range(WBLK // GROUP_W):
            gg = gz[:, g * GROUP_W:(g + 1) * GROUP_W]
            off = _V_SNW + b * WBLK + g * GROUP_W
            parts.append((gg * lax.rsqrt(jnp.mean(gg * gg, axis=-1, keepdims=True) + EPS)
                          * vec_ref[:, off:off + GROUP_W]).astype(BF16))
    y_s = _dot(jnp.concatenate(parts, axis=1), wso_ref[...])
    merged = acc_ref[...] + g_s * y_s
    xn = x_ref[...] + _dot(merged.astype(BF16), wo_ref[...])
    out_ref[...] = _rms(xn, _vec(vec_ref, _V_FNW, D_MODEL)) if final_norm else xn


def _prompt_ssd(x, acc, vec, w_all, wdt, csw, wso, wo, layer, tile, final_norm):
    bt, seq, _ = x.shape
    nc = seq // tile
    tok = lambda w: pl.BlockSpec((None, tile, w), lambda b, c: (b, c, 0))
    per_b = lambda r, w: pl.BlockSpec((None, r, w), lambda b, c: (b, 0, 0))
    blocks = (_B_SZ, _B_SZ + 1, _B_SX, _B_SX + 1, _B_SX + 2, _B_GS)
    return pl.pallas_call(
        functools.partial(_ssd_kernel, final_norm=final_norm),
        grid=(bt, nc),
        in_specs=([tok(D_MODEL), tok(D_MODEL), _vec_spec(layer)] + [_wblk(layer, b) for b in blocks]
                  + [_layer_mat((D_MODEL, LANES), layer), _layer_mat((4, SSM_XBC), layer),
                     _layer_mat((SSM_INNER, D_MODEL), layer), _layer_mat((D_MODEL, D_MODEL), layer)]),
        out_specs=[tok(D_MODEL), per_b(3, SSM_XBC), per_b(SSM_HEADS * SSM_HEADDIM, SSM_STATE)],
        out_shape=[jax.ShapeDtypeStruct((bt, seq, D_MODEL), F32),
                   jax.ShapeDtypeStruct((bt, 3, SSM_XBC), F32),
                   jax.ShapeDtypeStruct((bt, SSM_HEADS * SSM_HEADDIM, SSM_STATE), F32)],
        scratch_shapes=[pltpu.VMEM((STRIPS, tile + SUBLANES, LANES), F32),
                        pltpu.VMEM((SSM_XBC // LANES, SUBLANES, LANES), F32),
                        pltpu.VMEM((tile, SSM_XBC), F32),
                        pltpu.VMEM((tile, LANES), F32),
                        pltpu.VMEM((N_PAIRS, SSM_STATE, LANES), F32)],
        compiler_params=_params(("arbitrary", "arbitrary")),
        name="prompt_ssd",
    )(x, acc, vec, *([w_all] * len(blocks)), wdt, csw, wso, wo)


def _proj_kernel(x_ref, vec_ref, w_ref, o_ref):
    o_ref[...] = _dot(_rms(x_ref[...], _vec(vec_ref, _V_NW, D_MODEL)).astype(BF16), w_ref[...])


def _sample_proj(x, vec, w_all, layer, tn):
    m = x.shape[0]
    return pl.pallas_call(
        _proj_kernel,
        grid=(_S_END // tn,),
        in_specs=[_resident((m, D_MODEL), (0, 0)), _vec_spec(layer),
                  pl.BlockSpec((None, D_MODEL, tn), lambda j: (layer, 0, j))],
        out_specs=pl.BlockSpec((m, tn), lambda j: (0, j)),
        out_shape=jax.ShapeDtypeStruct((m, _S_END), F32),
        compiler_params=_params(("arbitrary",)),
        name="sample_proj",
    )(x, vec, w_all)


def _pre_kernel(p_ref, sca_ref, scs_ref, vec_ref, caw_ref, csw_ref,
                acta_ref, row_ref, xs_ref, nca_ref, ncs_ref):
    cb = p_ref[:, _S_A:_S_A + CONV_W]
    v = p_ref[:, _S_A + CONV_W:_S_A + 2 * CONV_W] * p_ref[:, _S_A + 2 * CONV_W:_S_A + 3 * CONV_W]
    cz = p_ref[:, _S_A + 3 * CONV_W:_S_A + 4 * CONV_W]
    caw = caw_ref[...]
    b0 = sca_ref[:, 0:CONV_W]
    b1 = sca_ref[:, CONV_W:2 * CONV_W]
    conv = caw[0:1] * b0 + caw[1:2] * b1 + caw[2:3] * v
    acta_ref[...] = _silu(cz) * cb * conv
    nca_ref[:, 0:CONV_W] = b1
    nca_ref[:, CONV_W:2 * CONV_W] = v

    sxbc = p_ref[:, _S_SX:_S_SX + SSM_XBC]
    csw = csw_ref[...]
    s0 = scs_ref[:, 0:SSM_XBC]
    s1 = scs_ref[:, SSM_XBC:2 * SSM_XBC]
    s2 = scs_ref[:, 2 * SSM_XBC:3 * SSM_XBC]
    xbc = _silu(csw[0:1] * s0 + csw[1:2] * s1 + csw[2:3] * s2 + csw[3:4] * sxbc + _vec(vec_ref, _V_CSB, SSM_XBC))
    ncs_ref[:, 0:SSM_XBC] = s1
    ncs_ref[:, SSM_XBC:2 * SSM_XBC] = s2
    ncs_ref[:, 2 * SSM_XBC:3 * SSM_XBC] = sxbc
    xs = xbc[:, :SSM_INNER]
    dt = _softplus(p_ref[:, _S_DT:_S_DT + SSM_INNER] + _vec(vec_ref, _V_DTBE, SSM_INNER))
    xs_ref[...] = xs
    row_ref[:, 0:_R_DA] = xs * dt
    row_ref[:, _R_DA:_R_B] = jnp.exp(dt * -jnp.exp(_vec(vec_ref, _V_ALOGE, SSM_INNER)))
    row_ref[:, _R_B:_R_END] = xbc[:, SSM_INNER:]


def _sample_pre(proj, sca, scs, vec, caw, csw, layer, rb):
    m = proj.shape[0]
    rows = lambda w: pl.BlockSpec((rb, w), lambda i: (i, 0))
    return pl.pallas_call(
        _pre_kernel,
        grid=(m // rb,),
        in_specs=[rows(_S_END), rows(2 * CONV_W), rows(3 * SSM_XBC), _vec_spec(layer),
                  _layer_mat((3, CONV_W), layer), _layer_mat((4, SSM_XBC), layer)],
        out_specs=[rows(CONV_W), rows(_R_END), rows(SSM_INNER), rows(2 * CONV_W), rows(3 * SSM_XBC)],
        out_shape=[jax.ShapeDtypeStruct((m, CONV_W), F32),
                   jax.ShapeDtypeStruct((m, _R_END), F32),
                   jax.ShapeDtypeStruct((m, SSM_INNER), F32),
                   jax.ShapeDtypeStruct((m, 2 * CONV_W), F32),
                   jax.ShapeDtypeStruct((m, 3 * SSM_XBC), F32)],
        compiler_params=_params(("arbitrary",)),
        name="sample_pre",
    )(proj, sca, scs, vec, caw, csw)


def _sssd_kernel(row_ref, h0_ref, *rest):
    hn_ref, y_ref = rest[-2:]
    nb = row_ref.shape[0]
    sub_x = lax.broadcasted_iota(jnp.int32, (LANES, GROUP_W), 0)
    sub_r = lax.broadcasted_iota(jnp.int32, (LANES, 2 * SSM_STATE), 0)
    sub_c = lax.broadcasted_iota(jnp.int32, (SUBLANES, SSM_STATE), 0)
    for i in range(nb):
        row = row_ref[i]
        ys = []
        for g in range(SSM_GROUPS):
            ch = slice(g * GROUP_W, (g + 1) * GROUP_W)
            xdt = row[:, ch]
            x_hi = xdt.astype(BF16).astype(F32)
            x_lo = xdt - x_hi
            da = row[:, _R_DA + g * GROUP_W:_R_DA + (g + 1) * GROUP_W]
            a_hi = da.astype(BF16).astype(F32)
            a_mid = (da - a_hi).astype(BF16).astype(F32)
            a_lo = da - a_hi - a_mid
            xk = jnp.where((sub_x == 0) | (sub_x == 2), x_hi,
                           jnp.where((sub_x == 1) | (sub_x == 3), x_lo,
                                     jnp.where(sub_x == 4, a_hi,
                                               jnp.where(sub_x == 5, a_mid,
                                                         jnp.where(sub_x == 6, a_lo, 0.0)))))
            bg = row[:, _R_B + g * SSM_STATE:_R_B + (g + 1) * SSM_STATE]
            b_hi = bg.astype(BF16).astype(F32)
            b_lo = bg - b_hi
            zero = jnp.zeros((1, SSM_STATE), F32)
            rk = jnp.where(sub_r < 2, jnp.concatenate([b_hi, zero], axis=1),
                           jnp.where(sub_r < 4, jnp.concatenate([b_lo, zero], axis=1),
                                     jnp.where(sub_r < 7, jnp.concatenate([zero, zero + 1.0], axis=1), 0.0)))
            bc = _dot(xk.T.astype(BF16), rk.astype(BF16))
            hnew = bc[:, SSM_STATE:] * h0_ref[i, ch, :] + bc[:, :SSM_STATE]
            hn_ref[i, ch, :] = hnew
            cg = row[:, _R_C + g * SSM_STATE:_R_C + (g + 1) * SSM_STATE]
            c8 = jnp.where(sub_c == 0, cg, 0.0).astype(BF16)
            ys.append(_dot_nt(c8, hnew.astype(BF16))[0:1])
        y_ref[i] = jnp.concatenate(ys, axis=1)


def _sample_ssd(rows, state_all, layer, stacked, nb):
    depth, m, hp, _ = state_all.shape
    state_spec = pl.BlockSpec((None, nb, hp, SSM_STATE), lambda i: (layer, i, 0, 0))
    in_specs = [pl.BlockSpec((nb, 1, _R_END), lambda i: (i, 0, 0)), state_spec]
    args = [rows.reshape(m, 1, _R_END), state_all]
    aliases = {}
    if stacked is not None:
        in_specs.append(pl.BlockSpec(memory_space=pl.ANY))
        args.append(stacked)
        aliases = {2: 0}
    return pl.pallas_call(
        _sssd_kernel,
        grid=(m // nb,),
        in_specs=in_specs,
        out_specs=[state_spec, pl.BlockSpec((nb, 1, hp), lambda i: (i, 0, 0))],
        out_shape=[jax.ShapeDtypeStruct((depth, m, hp, SSM_STATE), F32),
                   jax.ShapeDtypeStruct((m, 1, hp), F32)],
        input_output_aliases=aliases,
        compiler_params=_params(("arbitrary",)),
        name="sample_ssd",
    )(*args)


def _sattn_kernel(q_ref, k_ref, v_ref, o_ref):
    nb = q_ref.shape[0]
    halves = MEM_HEADDIM // LANES
    rows = halves * MEM_HEADS
    row = lax.broadcasted_iota(jnp.int32, (rows, CACHE_ROWS), 0)
    col = lax.broadcasted_iota(jnp.int32, (rows, CACHE_ROWS), 1)
    own = (col & (rows - 1)) == row
    lane = lax.broadcasted_iota(jnp.int32, (1, LANES), 1)
    low_half = (lane & MEM_HEADS) == 0
    for i in range(nb):
        q = q_ref[i]
        q8 = jnp.concatenate([q[:, (h * halves + j) * LANES:(h * halves + j + 1) * LANES]
                              for j in range(halves) for h in range(MEM_HEADS)], axis=0)
        s = _dot_nt(q8.astype(BF16), k_ref[i].astype(BF16))
        t = jnp.sum(jnp.where(own, s, 0.0), axis=0, keepdims=True)
        parts = []
        for b in range(CACHE_ROWS // LANES):
            tb = t[:, b * LANES:(b + 1) * LANES]
            parts.append(tb + jnp.where(low_half, pltpu.roll(tb, LANES - MEM_HEADS, 1),
                                        pltpu.roll(tb, MEM_HEADS, 1)))
        u = jnp.concatenate(parts, axis=1) * (MEM_HEADDIM ** -0.5)
        p = _softmax_rows(jnp.where(own, u, -jnp.inf))
        o8 = _dot(p.astype(BF16), v_ref[i].astype(BF16))
        for j in range(halves):
            for h in range(MEM_HEADS):
                r = j * MEM_HEADS + h
                o_ref[i, :, (h * halves + j) * LANES:(h * halves + j + 1) * LANES] = o8[r:r + 1, :]


def _cache_rows(c):
    depth, m = c.shape[:2]
    c = c.reshape(depth, m, N_MEM, MEM_HEADS, MEM_HEADDIM // LANES, LANES)
    return c.transpose(0, 1, 2, 4, 3, 5).reshape(depth, m, CACHE_ROWS, LANES)


def _sample_attn(q, k_rows, v_rows, layer, nb):
    m = q.shape[0]
    cache = pl.BlockSpec((None, nb, CACHE_ROWS, LANES), lambda i: (layer, i, 0, 0))
    return pl.pallas_call(
        _sattn_kernel,
        grid=(m // nb,),
        in_specs=[pl.BlockSpec((nb, 1, ATTN_W), lambda i: (i, 0, 0)), cache, cache],
        out_specs=pl.BlockSpec((nb, 1, ATTN_W), lambda i: (i, 0, 0)),
        out_shape=jax.ShapeDtypeStruct((m, 1, ATTN_W), F32),
        compiler_params=_params(("arbitrary",)),
        name="sample_attn",
    )(q.reshape(m, 1, ATTN_W), k_rows, v_rows)


def _post_kernel(x_ref, p_ref, acta_ref, xs_ref, ysd_ref, o_ref, vec_ref, wao_ref, wso_ref, wmo_ref, wo_ref,
                 out_ref, *, final_norm):
    y_a = _dot(acta_ref[...].astype(BF16), wao_ref[...])
    gz = ((ysd_ref[...] + _vec(vec_ref, _V_DSK, SSM_INNER) * xs_ref[...])
          * _silu(p_ref[:, _S_SZ:_S_SZ + SSM_INNER]))
    parts = []
    for g in range(SSM_GROUPS):
        gg = gz[:, g * GROUP_W:(g + 1) * GROUP_W]
        parts.append(gg * lax.rsqrt(jnp.mean(gg * gg, axis=-1, keepdims=True) + EPS))
    gn = jnp.concatenate(parts, axis=1) * _vec(vec_ref, _V_SNW, SSM_INNER)
    y_s = _dot(gn.astype(BF16), wso_ref[...])
    mz = p_ref[:, _S_M + ATTN_W:_S_M + 2 * ATTN_W]
    y_m = _dot((_silu(mz) * o_ref[...]).astype(BF16), wmo_ref[...])
    merged = (_sigmoid(p_ref[:, _S_G:_S_G + D_MODEL]) * y_a
              + _sigmoid(p_ref[:, _S_G + D_MODEL:_S_G + 2 * D_MODEL]) * y_s
              + _sigmoid(p_ref[:, _S_G + 2 * D_MODEL:_S_G + 3 * D_MODEL]) * y_m)
    xn = x_ref[...] + _dot(merged.astype(BF16), wo_ref[...])
    out_ref[...] = _rms(xn, _vec(vec_ref, _V_FNW, D_MODEL)) if final_norm else xn


def _sample_post(x, proj, acta, xs, ysd, o, vec, wao, wso, wmo, wo, layer, final_norm):
    m = x.shape[0]
    full = lambda a: _resident(a.shape, (0,) * a.ndim)
    acts = (x, proj, acta, xs, ysd, o)
    return pl.pallas_call(
        functools.partial(_post_kernel, final_norm=final_norm),
        grid=(1,),
        in_specs=([full(a) for a in acts]
                  + [_vec_spec(layer), _layer_mat((CONV_W, D_MODEL), layer), _layer_mat((SSM_INNER, D_MODEL), layer),
                     _layer_mat((ATTN_W, D_MODEL), layer), _layer_mat((D_MODEL, D_MODEL), layer)]),
        out_specs=_resident((m, D_MODEL), (0, 0)),
        out_shape=jax.ShapeDtypeStruct((m, D_MODEL), F32),
        compiler_params=_params(("arbitrary",)),
        name="sample_post",
    )(*acts, vec, wao, wso, wmo, wo)


def _pad_lanes(a):
    return jnp.pad(a, ((0, 0), (0, LANES - a.shape[1])))


def _prompt_tile(seq):
    for t in (512, 256, 128):
        if seq % t == 0:
            return t
    raise ValueError(f"sequence length {seq} must be a multiple of {SSD_CHUNK}")


def kernel(x_prompt, x_sample, mem_prompt, state_conv_a, state_conv_s, state_ssm, cache_mem_k, cache_mem_v,
           norm_w, w_in, conv_a_w, w_a_out, conv_s_w, conv_s_b, dt_bias, a_log, d_skip, ssm_norm_w,
           w_s_out, mem_norm_w, w_mem_kv, w_m_out, w_o, final_norm_w):
    depth = w_in.shape[0]
    bp, seq, _ = x_prompt.shape
    bs = x_sample.shape[0]
    assert x_sample.shape[1] == 1
    tile = _prompt_tile(seq)

    wb = w_in.astype(BF16)
    wdt = wb[:, :, _OFF_DT:_OFF_M]
    w_all = jnp.concatenate([wb[:, :, :_OFF_DT], jnp.repeat(wdt, SSM_HEADDIM, axis=2), wb[:, :, _OFF_M:]], axis=2)
    wdt = jnp.pad(wdt, ((0, 0), (0, 0), (0, LANES - SSM_HEADS)))
    wao, wso, wmo, wo, wkv = (a.astype(BF16) for a in (w_a_out, w_s_out, w_m_out, w_o, w_mem_kv))
    rep = lambda a: jnp.repeat(a, SSM_HEADDIM, axis=1)
    vec = jnp.concatenate([norm_w, mem_norm_w, conv_s_b, _pad_lanes(dt_bias), _pad_lanes(a_log), rep(d_skip),
                           ssm_norm_w, rep(dt_bias), rep(a_log),
                           jnp.broadcast_to(final_norm_w[None], (depth, D_MODEL))], axis=1)[:, None, :]

    x = x_prompt
    pa, ps, ph, pk, pv = [], [], [], [], []
    for l in range(depth):
        last = l == depth - 1
        k, v, kt, vb = _mem_kv(mem_prompt, vec, wkv, l)
        acc, ca = _prompt_am(x, vec, w_all, conv_a_w, wao, wmo, kt, vb, l, tile)
        x, cs, hs = _prompt_ssd(x, acc, vec, w_all, wdt, conv_s_w, wso, wo, l, tile, last)
        pa.append(ca)
        ps.append(cs)
        ph.append(hs.reshape(bp, SSM_HEADS, SSM_HEADDIM, SSM_STATE))
        pk.append(k.reshape(bp, N_MEM, MEM_HEADS, MEM_HEADDIM))
        pv.append(v.reshape(bp, N_MEM, MEM_HEADS, MEM_HEADDIM))
    y_prompt = x

    x = x_sample.reshape(bs, D_MODEL)
    state_all = state_ssm.reshape(depth, bs, SSM_HEADS * SSM_HEADDIM, SSM_STATE)
    k_rows, v_rows = _cache_rows(cache_mem_k), _cache_rows(cache_mem_v)
    sa, ss, s_ssm = [], [], None
    for l in range(depth):
        last = l == depth - 1
        proj = _sample_proj(x, vec, w_all, l, 2048)
        acta, rows, xs, nca, ncs = _sample_pre(
            proj, state_conv_a[l].reshape(bs, 2 * CONV_W), state_conv_s[l].reshape(bs, 3 * SSM_XBC),
            vec, conv_a_w, conv_s_w, l, 32)
        s_ssm, ysd = _sample_ssd(rows, state_all, l, s_ssm, 2)
        o = _sample_attn(proj[:, _S_M:_S_M + ATTN_W], k_rows, v_rows, l, 2)
        x = _sample_post(x, proj, acta, xs, ysd.reshape(bs, SSM_INNER), o.reshape(bs, ATTN_W), vec,
                         wao, wso, wmo, wo, l, last)
        sa.append(nca.reshape(bs, 2, CONV_W))
        ss.append(ncs.reshape(bs, 3, SSM_XBC))
    y_sample = x.reshape(bs, 1, D_MODEL)

    return (y_prompt, y_sample, jnp.stack(pa), jnp.stack(ps), jnp.stack(ph), jnp.stack(pk), jnp.stack(pv),
            jnp.stack(sa), jnp.stack(ss), s_ssm.reshape(depth, bs, SSM_HEADS, SSM_HEADDIM, SSM_STATE))
```

```python
import functools

import jax
import jax.numpy as jnp
from jax import lax
from jax.experimental import pallas as pl
from jax.experimental.pallas import tpu as pltpu

F32 = jnp.float32
BF16 = jnp.bfloat16

D_MODEL = 1024
CONV_W = 1024
SSM_INNER = 2048
SSM_HEADDIM = 64
SSM_HEADS = 32
SSM_GROUPS = 4
SSM_STATE = 128
SSM_XBC = SSM_INNER + 2 * SSM_GROUPS * SSM_STATE
HEADS_PER_GROUP = SSM_HEADS // SSM_GROUPS
GROUP_W = SSM_INNER // SSM_GROUPS
N_PAIRS = SSM_HEADS // 2
N_MEM = 256
MEM_HEADS = 4
MEM_HEADDIM = 256
ATTN_W = MEM_HEADS * MEM_HEADDIM
EPS = 1e-6
LOG2E = 1.4426950408889634
SSD_CHUNK = 128
LANES = 128
SUBLANES = 8
CACHE_ROWS = N_MEM * ATTN_W // LANES
VMEM_LIMIT = 58 * 1024 * 1024
WBLK = 1024
SIDE_W = 256
SSD_SEQS_PER_STEP = 4
ATTN_SEQS_PER_STEP = 8
STRIPS = WBLK // LANES

_B_CB, _B_CC, _B_CH, _B_CZ = 0, 1, 2, 3
_B_SZ = 4
_B_SX = 6
_B_DT = 9
_B_Q, _B_MZ = 11, 12
_B_GA, _B_GS, _B_GM = 13, 14, 15
_N_BLK = 16
_S_A = 0
_S_SZ = _B_SZ * WBLK
_S_SX = _B_SX * WBLK
_S_DT = _B_DT * WBLK
_S_M = _B_Q * WBLK
_S_G = _B_GA * WBLK
_S_END = _N_BLK * WBLK
_OFF_DT = 4 * CONV_W + SSM_INNER + SSM_XBC
_OFF_M = _OFF_DT + SSM_HEADS
_R_DA = SSM_INNER
_R_B = 2 * SSM_INNER
_R_C = _R_B + SSM_GROUPS * SSM_STATE
_R_END = _R_C + SSM_GROUPS * SSM_STATE

_V_NW = 0
_V_MNW = _V_NW + D_MODEL
_V_CSB = _V_MNW + D_MODEL
_V_DTB = _V_CSB + SSM_XBC
_V_ALOG = _V_DTB + LANES
_V_DSK = _V_ALOG + LANES
_V_SNW = _V_DSK + SSM_INNER
_V_DTBE = _V_SNW + SSM_INNER
_V_ALOGE = _V_DTBE + SSM_INNER
_V_FNW = _V_ALOGE + SSM_INNER
_V_END = _V_FNW + D_MODEL


def _dot(a, b):
    return jnp.dot(a, b, preferred_element_type=F32)


def _dot_nt(a, b):
    return lax.dot_general(a, b, (((1,), (1,)), ((), ())), preferred_element_type=F32)


def _rms(x, w):
    return x * lax.rsqrt(jnp.mean(x * x, axis=-1, keepdims=True) + EPS) * w


def _sigmoid(x):
    return 0.5 * jnp.tanh(0.5 * x) + 0.5


def _silu(x):
    h = 0.5 * x
    return h * jnp.tanh(h) + h


def _softplus(x):
    return jnp.maximum(x, 0.0) + jnp.log1p(jnp.exp(-jnp.abs(x)))


def _softmax_rows(s):
    m = jnp.max(s, axis=-1, keepdims=True)
    e = jnp.exp(s - m)
    return e / jnp.sum(e, axis=-1, keepdims=True)


def _split3(x):
    hi = x.astype(BF16)
    r1 = x - hi.astype(F32)
    mid = r1.astype(BF16)
    lo = (r1 - mid.astype(F32)).astype(BF16)
    return hi, mid, lo


def _vec(vec_ref, off, width):
    return vec_ref[:, off:off + width]


def _resident(block, index):
    return pl.BlockSpec(block, lambda *_: index, pipeline_mode=pl.Buffered(1))


_SRC_FIRST = (0, _B_DT, _B_Q)


def _wsrc(blk):
    src = max(i for i, first in enumerate(_SRC_FIRST) if blk >= first)
    return src, blk - _SRC_FIRST[src]


def _wblk(layer, blk):
    return _resident((None, D_MODEL, WBLK), (layer, 0, _wsrc(blk)[1]))


def _wargs(wsrc, blocks):
    return [wsrc[_wsrc(b)[0]] for b in blocks]


def _vec_spec(layer):
    return _resident((None, 1, _V_END), (layer, 0, 0))


def _layer_mat(shape, layer):
    return _resident((None,) + shape, (layer, 0, 0))


def _params(sem):
    return pltpu.CompilerParams(dimension_semantics=sem, vmem_limit_bytes=VMEM_LIMIT)


def _kv_kernel(mem_ref, vec_ref, w_ref, k_ref, v_ref, kt_ref, vb_ref):
    u = _rms(mem_ref[...], _vec(vec_ref, _V_MNW, D_MODEL)).astype(BF16)
    kv = _dot(u, w_ref[...])
    k = kv[:, :ATTN_W]
    v = kv[:, ATTN_W:]
    halves = MEM_HEADDIM // LANES
    for j in range(halves):
        for h in range(MEM_HEADS):
            rows = pl.ds(j * MEM_HEADS + h, N_MEM, stride=halves * MEM_HEADS)
            sl = slice((h * halves + j) * LANES, (h * halves + j + 1) * LANES)
            k_ref[rows, :] = k[:, sl]
            v_ref[rows, :] = v[:, sl]
    kt_ref[...] = (k.T * (MEM_HEADDIM ** -0.5)).astype(BF16)
    vb_ref[...] = v.astype(BF16)


def _mem_kv(mem, vec, w_kv):
    depth = w_kv.shape[0]
    bt = mem.shape[0]
    out = lambda r, c: pl.BlockSpec((None, None, r, c), lambda l, b: (l, b, 0, 0))
    per_layer = lambda r, c: pl.BlockSpec((None, r, c), lambda l, b: (l, 0, 0))
    return pl.pallas_call(
        _kv_kernel,
        grid=(depth, bt),
        in_specs=[pl.BlockSpec((None, N_MEM, D_MODEL), lambda l, b: (b, 0, 0)), per_layer(1, _V_END),
                  per_layer(D_MODEL, 2 * ATTN_W)],
        out_specs=[out(CACHE_ROWS, LANES), out(CACHE_ROWS, LANES), out(ATTN_W, N_MEM), out(N_MEM, ATTN_W)],
        out_shape=[jax.ShapeDtypeStruct((depth, bt, CACHE_ROWS, LANES), F32),
                   jax.ShapeDtypeStruct((depth, bt, CACHE_ROWS, LANES), F32),
                   jax.ShapeDtypeStruct((depth, bt, ATTN_W, N_MEM), BF16),
                   jax.ShapeDtypeStruct((depth, bt, N_MEM, ATTN_W), BF16)],
        compiler_params=_params(("arbitrary", "arbitrary")),
        name="prompt_kv",
    )(mem, vec, w_kv)


def _cache_from_rows(c):
    depth, m = c.shape[:2]
    c = c.reshape(depth, m, N_MEM, MEM_HEADDIM // LANES, MEM_HEADS, LANES)
    return c.transpose(0, 1, 2, 4, 3, 5).reshape(depth, m, N_MEM, MEM_HEADS, MEM_HEADDIM)


def _am_kernel(x_ref, vec_ref, wcb_ref, wcc_ref, wch_ref, wcz_ref, wq_ref, wmz_ref, wga_ref, wgm_ref,
               caw_ref, wao_ref, wmo_ref, kt_ref, vb_ref,
               acc_ref, ca_ref, vbuf):
    c = pl.program_id(1)
    t = x_ref.shape[0]
    pad = SUBLANES

    @pl.when(c == 0)
    def _():
        vbuf[:, 0:pad, :] = jnp.zeros((STRIPS, pad, LANES), F32)

    ub = _rms(x_ref[...], _vec(vec_ref, _V_NW, D_MODEL)).astype(BF16)

    v = _dot(ub, wcc_ref[...]) * _dot(ub, wch_ref[...])
    gate = _silu(_dot(ub, wcz_ref[...])) * _dot(ub, wcb_ref[...])
    caw = caw_ref[...]
    acts = []
    for j in range(STRIPS):
        sl = slice(j * LANES, (j + 1) * LANES)
        vj = v[:, sl]
        vbuf[j, pad:pad + t, :] = vj
        conv = (caw[0:1, sl] * vbuf[j, pad - 2:pad - 2 + t, :] + caw[1:2, sl] * vbuf[j, pad - 1:pad - 1 + t, :]
                + caw[2:3, sl] * vj)
        acts.append((gate[:, sl] * conv).astype(BF16))
        ca_ref[:, sl] = vbuf[j, pad + t - 2:pad + t, :]
        vbuf[j, 0:pad, :] = vbuf[j, t:t + pad, :]
    y_a = _dot(jnp.concatenate(acts, axis=1), wao_ref[...])

    q = _dot(ub, wq_ref[...])
    mz = _dot(ub, wmz_ref[...])
    acts = []
    for h in range(MEM_HEADS):
        sl = slice(h * MEM_HEADDIM, (h + 1) * MEM_HEADDIM)
        p = _softmax_rows(_dot(q[:, sl].astype(BF16), kt_ref[sl, :]))
        o = _dot(p.astype(BF16), vb_ref[:, sl])
        acts.append((_silu(mz[:, sl]) * o).astype(BF16))
    y_m = _dot(jnp.concatenate(acts, axis=1), wmo_ref[...])

    acc_ref[...] = _sigmoid(_dot(ub, wga_ref[...])) * y_a + _sigmoid(_dot(ub, wgm_ref[...])) * y_m


def _prompt_am(x, vec, w_all, caw, wao, wmo, kt, vb, layer, tile):
    bt, seq, _ = x.shape
    nc = seq // tile
    tok = lambda w: pl.BlockSpec((None, tile, w), lambda b, c: (b, c, 0))
    per_b = lambda r, w: pl.BlockSpec((None, r, w), lambda b, c: (b, 0, 0))
    mem_kv = lambda r, w: pl.BlockSpec((None, None, r, w), lambda b, c: (layer, b, 0, 0))
    blocks = (_B_CB, _B_CC, _B_CH, _B_CZ, _B_Q, _B_MZ, _B_GA, _B_GM)
    return pl.pallas_call(
        _am_kernel,
        grid=(bt, nc),
        in_specs=([tok(D_MODEL), _vec_spec(layer)] + [_wblk(layer, b) for b in blocks]
                  + [_layer_mat((3, CONV_W), layer), _layer_mat((CONV_W, D_MODEL), layer),
                     _layer_mat((ATTN_W, D_MODEL), layer), mem_kv(ATTN_W, N_MEM), mem_kv(N_MEM, ATTN_W)]),
        out_specs=[tok(D_MODEL), per_b(2, CONV_W)],
        out_shape=[jax.ShapeDtypeStruct((bt, seq, D_MODEL), F32),
                   jax.ShapeDtypeStruct((bt, 2, CONV_W), F32)],
        scratch_shapes=[pltpu.VMEM((STRIPS, tile + SUBLANES, LANES), F32)],
        compiler_params=_params(("arbitrary", "arbitrary")),
        name="prompt_am",
    )(x, vec, *_wargs(w_all, blocks), caw, wao, wmo, kt, vb)


def _cumsum_rows(da):
    q = da.shape[0]
    r = lax.broadcasted_iota(jnp.int32, (q, q), 0)
    c = lax.broadcasted_iota(jnp.int32, (q, q), 1)
    tri = jnp.where(r >= c, 1.0, 0.0).astype(BF16)
    hi, mid, lo = _split3(da)
    return _dot(tri, hi) + _dot(tri, mid) + _dot(tri, lo)


def _ssd_kernel(x_ref, acc_ref, vec_ref, wsz0_ref, wsz1_ref, wsx0_ref, wsx1_ref, wsx2_ref, wgs_ref, wdt_ref,
                csw_ref, wso_ref, wo_ref,
                out_ref, cs_ref, ssm_ref,
                xbuf, tails, xbc_scr, dt_scr, ht_scr, *, final_norm):
    c = pl.program_id(1)
    t = x_ref.shape[0]
    q = SSD_CHUNK
    pad = SUBLANES

    @pl.when(c == 0)
    def _():
        tails[...] = jnp.zeros(tails.shape, F32)
        ht_scr[...] = jnp.zeros(ht_scr.shape, F32)

    ub = _rms(x_ref[...], _vec(vec_ref, _V_NW, D_MODEL)).astype(BF16)

    csw = csw_ref[...]
    for b, w_ref in enumerate((wsx0_ref, wsx1_ref, wsx2_ref)):
        sx = _dot(ub, w_ref[...])
        for j in range(STRIPS):
            s = b * STRIPS + j
            sl = slice(s * LANES, (s + 1) * LANES)
            sxj = sx[:, j * LANES:(j + 1) * LANES]
            xbuf[j, 0:pad, :] = tails[s]
            xbuf[j, pad:pad + t, :] = sxj
            yc = (csw[0:1, sl] * xbuf[j, pad - 3:pad - 3 + t, :] + csw[1:2, sl] * xbuf[j, pad - 2:pad - 2 + t, :]
                  + csw[2:3, sl] * xbuf[j, pad - 1:pad - 1 + t, :] + csw[3:4, sl] * sxj
                  + vec_ref[:, _V_CSB + s * LANES:_V_CSB + (s + 1) * LANES])
            xbc_scr[:, sl] = _silu(yc)
            cs_ref[:, sl] = xbuf[j, pad + t - 3:pad + t, :]
            tails[s] = xbuf[j, t:t + pad, :]
    dt_scr[...] = _softplus(_dot(ub, wdt_ref[...]) + _vec(vec_ref, _V_DTB, LANES))
    a_row = -jnp.exp(_vec(vec_ref, _V_ALOG, LANES))

    rr = lax.broadcasted_iota(jnp.int32, (q, q), 0)
    cc = lax.broadcasted_iota(jnp.int32, (q, q), 1)
    tril = rr >= cc
    left = lax.broadcasted_iota(jnp.int32, (1, LANES), 1) < SSM_HEADDIM

    def chunk(j, after_pair):
        rows = slice(j * q, (j + 1) * q)
        dtj = dt_scr[rows, :]
        cs = _cumsum_rows(dtj * a_row) * LOG2E
        cs_t = cs.T
        dt_t = dtj.T
        w_t = dt_t * jnp.exp2(cs_t[:, q - 1:q] - cs_t)
        etot = jnp.exp2(cs[q - 1:q, :])

        scores, bm_t, cms = [], [], []
        for g in range(SSM_GROUPS):
            bg = xbc_scr[rows, SSM_INNER + g * SSM_STATE:SSM_INNER + (g + 1) * SSM_STATE]
            cg = xbc_scr[rows, SSM_INNER + (SSM_GROUPS + g) * SSM_STATE:
                         SSM_INNER + (SSM_GROUPS + g + 1) * SSM_STATE]
            scores.append(_dot_nt(cg.astype(BF16), bg.astype(BF16)))
            bm_t.append(bg.T)
            cms.append(cg)

        for pair in range(N_PAIRS):
            g = (2 * pair) // HEADS_PER_GROUP
            lanes = slice(pair * LANES, (pair + 1) * LANES)
            xs_p = xbc_scr[rows, lanes]
            ht = ht_scr[pair]
            xsb = xs_p.astype(BF16)
            rhs = jnp.concatenate([xsb, ht.astype(BF16)], axis=0)
            ys, upds = [], []
            for h in (2 * pair, 2 * pair + 1):
                col = jnp.broadcast_to(cs[:, h:h + 1], (q, q))
                row = jnp.broadcast_to(cs_t[h:h + 1, :], (q, q))
                lmat = jnp.exp2(jnp.where(tril, col - row, -jnp.inf))
                lhs = jnp.concatenate([(scores[g] * lmat * dt_t[h:h + 1, :]).astype(BF16),
                                       (cms[g] * jnp.exp2(col)).astype(BF16)], axis=1)
                ys.append(_dot(lhs, rhs))
                upds.append(_dot((bm_t[g] * w_t[h:h + 1, :]).astype(BF16), xsb))
            xbc_scr[rows, lanes] = (jnp.where(left, ys[0], ys[1])
                                    + vec_ref[:, _V_DSK + pair * LANES:_V_DSK + (pair + 1) * LANES] * xs_p)
            dec = jnp.where(left, jnp.broadcast_to(etot[:, 2 * pair:2 * pair + 1], (1, LANES)),
                            jnp.broadcast_to(etot[:, 2 * pair + 1:2 * pair + 2], (1, LANES)))
            ht_scr[pair] = ht * dec + jnp.where(left, upds[0], upds[1])
            after_pair(j * N_PAIRS + pair)

    jobs = [(w_ref, k, act) for w_ref, act in ((wsz0_ref, _silu), (wsz1_ref, _silu), (wgs_ref, _sigmoid))
            for k in range(WBLK // SIDE_W)]
    n_slots = (t // q) * N_PAIRS
    job_at = {((2 * i + 1) * n_slots) // (2 * len(jobs)): i for i in range(len(jobs))}
    assert len(job_at) == len(jobs)
    side_vals = []

    def after_pair(slot):
        if slot in job_at:
            w_ref, k, act = jobs[job_at[slot]]
            side_vals.append(act(_dot(ub, w_ref[:, k * SIDE_W:(k + 1) * SIDE_W])))

    for j in range(t // q):
        chunk(j, after_pair)
    per_blk = WBLK // SIDE_W
    zgate = [jnp.concatenate(side_vals[b * per_blk:(b + 1) * per_blk], axis=1) for b in range(2)]
    g_s = jnp.concatenate(side_vals[2 * per_blk:], axis=1)

    @pl.when(c == pl.num_programs(1) - 1)
    def _():
        for pair in range(N_PAIRS):
            ssm_ref[pair * LANES:(pair + 1) * LANES, :] = ht_scr[pair].T

    parts = []
    for b in range(SSM_INNER // WBLK):
        gz = xbc_scr[:, b * WBLK:(b + 1) * WBLK] * zgate[b]
        for g in range(WBLK // GROUP_W):
            gg = gz[:, g * GROUP_W:(g + 1) * GROUP_W]
            off = _V_SNW + b * WBLK + g * GROUP_W
            parts.append((gg * lax.rsqrt(jnp.mean(gg * gg, axis=-1, keepdims=True) + EPS)
                          * vec_ref[:, off:off + GROUP_W]).astype(BF16))
    y_s = _dot(jnp.concatenate(parts, axis=1), wso_ref[...])
    merged = acc_ref[...] + g_s * y_s
    xn = x_ref[...] + _dot(merged.astype(BF16), wo_ref[...])
    out_ref[...] = _rms(xn, _vec(vec_ref, _V_FNW, D_MODEL)) if final_norm else xn


def _prompt_ssd(x, acc, vec, w_all, wdt, csw, wso, wo, layer, tile, final_norm):
    bt, seq, _ = x.shape
    nc = seq // tile
    tok = lambda w: pl.BlockSpec((None, tile, w), lambda b, c: (b, c, 0))
    per_b = lambda r, w: pl.BlockSpec((None, r, w), lambda b, c: (b, 0, 0))
    blocks = (_B_SZ, _B_SZ + 1, _B_SX, _B_SX + 1, _B_SX + 2, _B_GS)
    return pl.pallas_call(
        functools.partial(_ssd_kernel, final_norm=final_norm),
        grid=(bt, nc),
        in_specs=([tok(D_MODEL), tok(D_MODEL), _vec_spec(layer)] + [_wblk(layer, b) for b in blocks]
                  + [_layer_mat((D_MODEL, LANES), layer), _layer_mat((4, SSM_XBC), layer),
                     _layer_mat((SSM_INNER, D_MODEL), layer), _layer_mat((D_MODEL, D_MODEL), layer)]),
        out_specs=[tok(D_MODEL), per_b(3, SSM_XBC), per_b(SSM_HEADS * SSM_HEADDIM, SSM_STATE)],
        out_shape=[jax.ShapeDtypeStruct((bt, seq, D_MODEL), F32),
                   jax.ShapeDtypeStruct((bt, 3, SSM_XBC), F32),
                   jax.ShapeDtypeStruct((bt, SSM_HEADS * SSM_HEADDIM, SSM_STATE), F32)],
        scratch_shapes=[pltpu.VMEM((STRIPS, tile + SUBLANES, LANES), F32),
                        pltpu.VMEM((SSM_XBC // LANES, SUBLANES, LANES), F32),
                        pltpu.VMEM((tile, SSM_XBC), F32),
                        pltpu.VMEM((tile, LANES), F32),
                        pltpu.VMEM((N_PAIRS, SSM_STATE, LANES), F32)],
        compiler_params=_params(("arbitrary", "arbitrary")),
        name="prompt_ssd",
    )(x, acc, vec, *_wargs(w_all, blocks), wdt, csw, wso, wo)


def _proj_kernel(x_ref, vec_ref, *refs):
    w_refs, o_ref = refs[:-1], refs[-1]
    j = pl.program_id(0)
    ub = _rms(x_ref[...], _vec(vec_ref, _V_NW, D_MODEL)).astype(BF16)
    bounds = _SRC_FIRST + (_N_BLK,)
    for src, w_ref in enumerate(w_refs):
        @pl.when((j >= bounds[src]) & (j < bounds[src + 1]))
        def _(w_ref=w_ref):
            o_ref[...] = _dot(ub, w_ref[...])


def _sample_proj(x, vec, w_all, layer):
    m = x.shape[0]
    bounds = _SRC_FIRST + (_N_BLK,)

    def wspec(src):
        first, count = bounds[src], bounds[src + 1] - bounds[src]
        return pl.BlockSpec((None, D_MODEL, WBLK), lambda j: (layer, 0, jnp.clip(j - first, 0, count - 1)))

    return pl.pallas_call(
        _proj_kernel,
        grid=(_N_BLK,),
        in_specs=[_resident((m, D_MODEL), (0, 0)), _vec_spec(layer)] + [wspec(s) for s in range(len(w_all))],
        out_specs=pl.BlockSpec((m, WBLK), lambda j: (0, j)),
        out_shape=jax.ShapeDtypeStruct((m, _S_END), F32),
        compiler_params=_params(("arbitrary",)),
        name="sample_proj",
    )(x, vec, *w_all)


def _pre_kernel(p_ref, sca_ref, scs_ref, vec_ref, caw_ref, csw_ref,
                acta_ref, row_ref, xs_ref, nca_ref, ncs_ref):
    cb = p_ref[:, _S_A:_S_A + CONV_W]
    v = p_ref[:, _S_A + CONV_W:_S_A + 2 * CONV_W] * p_ref[:, _S_A + 2 * CONV_W:_S_A + 3 * CONV_W]
    cz = p_ref[:, _S_A + 3 * CONV_W:_S_A + 4 * CONV_W]
    caw = caw_ref[...]
    b0 = sca_ref[:, 0:CONV_W]
    b1 = sca_ref[:, CONV_W:2 * CONV_W]
    conv = caw[0:1] * b0 + caw[1:2] * b1 + caw[2:3] * v
    acta_ref[...] = _silu(cz) * cb * conv
    nca_ref[:, 0:CONV_W] = b1
    nca_ref[:, CONV_W:2 * CONV_W] = v

    sxbc = p_ref[:, _S_SX:_S_SX + SSM_XBC]
    csw = csw_ref[...]
    s0 = scs_ref[:, 0:SSM_XBC]
    s1 = scs_ref[:, SSM_XBC:2 * SSM_XBC]
    s2 = scs_ref[:, 2 * SSM_XBC:3 * SSM_XBC]
    xbc = _silu(csw[0:1] * s0 + csw[1:2] * s1 + csw[2:3] * s2 + csw[3:4] * sxbc + _vec(vec_ref, _V_CSB, SSM_XBC))
    ncs_ref[:, 0:SSM_XBC] = s1
    ncs_ref[:, SSM_XBC:2 * SSM_XBC] = s2
    ncs_ref[:, 2 * SSM_XBC:3 * SSM_XBC] = sxbc
    xs = xbc[:, :SSM_INNER]
    dt = _softplus(p_ref[:, _S_DT:_S_DT + SSM_INNER] + _vec(vec_ref, _V_DTBE, SSM_INNER))
    xs_ref[...] = xs
    row_ref[:, 0:_R_DA] = xs * dt
    row_ref[:, _R_DA:_R_B] = jnp.exp(dt * -jnp.exp(_vec(vec_ref, _V_ALOGE, SSM_INNER)))
    row_ref[:, _R_B:_R_END] = xbc[:, SSM_INNER:]


def _sample_pre(proj, sca, scs, vec, caw, csw, layer, rb):
    m = proj.shape[0]
    rows = lambda w: pl.BlockSpec((rb, w), lambda i: (i, 0))
    return pl.pallas_call(
        _pre_kernel,
        grid=(m // rb,),
        in_specs=[rows(_S_END), rows(2 * CONV_W), rows(3 * SSM_XBC), _vec_spec(layer),
                  _layer_mat((3, CONV_W), layer), _layer_mat((4, SSM_XBC), layer)],
        out_specs=[rows(CONV_W), rows(_R_END), rows(SSM_INNER), rows(2 * CONV_W), rows(3 * SSM_XBC)],
        out_shape=[jax.ShapeDtypeStruct((m, CONV_W), F32),
                   jax.ShapeDtypeStruct((m, _R_END), F32),
                   jax.ShapeDtypeStruct((m, SSM_INNER), F32),
                   jax.ShapeDtypeStruct((m, 2 * CONV_W), F32),
                   jax.ShapeDtypeStruct((m, 3 * SSM_XBC), F32)],
        compiler_params=_params(("arbitrary",)),
        name="sample_pre",
    )(proj, sca, scs, vec, caw, csw)


def _sssd_kernel(row_ref, h0_ref, *rest):
    hn_ref, y_ref = rest[-2:]
    nb = row_ref.shape[0]
    sub_x = lax.broadcasted_iota(jnp.int32, (LANES, GROUP_W), 0)
    sub_r = lax.broadcasted_iota(jnp.int32, (LANES, 2 * SSM_STATE), 0)
    eye = (lax.broadcasted_iota(jnp.int32, (LANES, SSM_STATE), 0)
           == lax.broadcasted_iota(jnp.int32, (LANES, SSM_STATE), 1))
    zero = jnp.zeros((1, SSM_STATE), F32)
    for i in range(nb):
        row = row_ref[i]
        xparts = []
        rk = jnp.zeros((LANES, 2 * SSM_STATE), F32)
        for g in range(SSM_GROUPS):
            xdt = row[:, g * GROUP_W:(g + 1) * GROUP_W]
            x_hi = xdt.astype(BF16).astype(F32)
            x_lo = xdt - x_hi
            da = row[:, _R_DA + g * GROUP_W:_R_DA + (g + 1) * GROUP_W]
            a_hi = da.astype(BF16).astype(F32)
            a_mid = (da - a_hi).astype(BF16).astype(F32)
            a_lo = da - a_hi - a_mid
            k = sub_x - g * SUBLANES
            xparts.append(jnp.where((k == 0) | (k == 2), x_hi,
                                    jnp.where((k == 1) | (k == 3), x_lo,
                                              jnp.where(k == 4, a_hi,
                                                        jnp.where(k == 5, a_mid,
                                                                  jnp.where(k == 6, a_lo, 0.0))))))
            bg = row[:, _R_B + g * SSM_STATE:_R_B + (g + 1) * SSM_STATE]
            b_hi = bg.astype(BF16).astype(F32)
            b_lo = bg - b_hi
            kr = sub_r - g * SUBLANES
            rk = jnp.where((kr == 0) | (kr == 1), jnp.concatenate([b_hi, zero], axis=1),
                           jnp.where((kr == 2) | (kr == 3), jnp.concatenate([b_lo, zero], axis=1),
                                     jnp.where((kr >= 4) & (kr < 7), jnp.concatenate([zero, zero + 1.0], axis=1),
                                               rk)))
        xk = jnp.concatenate(xparts, axis=1)
        bc = _dot(xk.T.astype(BF16), rk.astype(BF16))
        hnew = bc[:, SSM_STATE:] * h0_ref[i] + bc[:, :SSM_STATE]
        hn_ref[i] = hnew
        hb = hnew.astype(BF16)
        ys = []
        for g in range(SSM_GROUPS):
            cg = row[:, _R_C + g * SSM_STATE:_R_C + (g + 1) * SSM_STATE]
            c_cols = jnp.broadcast_to(cg, (SSM_STATE, SSM_STATE)).T.astype(BF16)
            rep = _dot(hb[g * GROUP_W:(g + 1) * GROUP_W, :], c_cols)
            for k in range(GROUP_W // LANES):
                ys.append(jnp.sum(jnp.where(eye, rep[k * LANES:(k + 1) * LANES, :], 0.0), axis=0, keepdims=True))
        y_ref[i] = jnp.concatenate(ys, axis=1)


def _sample_ssd(rows, state_all, layer, stacked, nb):
    depth, m, hp, _ = state_all.shape
    state_spec = pl.BlockSpec((None, nb, hp, SSM_STATE), lambda i: (layer, i, 0, 0))
    in_specs = [pl.BlockSpec((nb, 1, _R_END), lambda i: (i, 0, 0)), state_spec]
    args = [rows.reshape(m, 1, _R_END), state_all]
    aliases = {}
    if stacked is not None:
        in_specs.append(pl.BlockSpec(memory_space=pl.ANY))
        args.append(stacked)
        aliases = {2: 0}
    return pl.pallas_call(
        _sssd_kernel,
        grid=(m // nb,),
        in_specs=in_specs,
        out_specs=[state_spec, pl.BlockSpec((nb, 1, hp), lambda i: (i, 0, 0))],
        out_shape=[jax.ShapeDtypeStruct((depth, m, hp, SSM_STATE), F32),
                   jax.ShapeDtypeStruct((m, 1, hp), F32)],
        input_output_aliases=aliases,
        compiler_params=_params(("arbitrary",)),
        name="sample_ssd",
    )(*args)


def _sattn_kernel(q_ref, k_ref, v_ref, o_ref):
    nb = q_ref.shape[0]
    halves = MEM_HEADDIM // LANES
    rows = halves * MEM_HEADS
    row = lax.broadcasted_iota(jnp.int32, (rows, CACHE_ROWS), 0)
    col = lax.broadcasted_iota(jnp.int32, (rows, CACHE_ROWS), 1)
    own = (col & (rows - 1)) == row
    lane = lax.broadcasted_iota(jnp.int32, (1, LANES), 1)
    low_half = (lane & MEM_HEADS) == 0
    for i in range(nb):
        q = q_ref[i]
        q8 = jnp.concatenate([q[:, (h * halves + j) * LANES:(h * halves + j + 1) * LANES]
                              for j in range(halves) for h in range(MEM_HEADS)], axis=0)
        s = _dot_nt(q8.astype(BF16), k_ref[i].astype(BF16))
        t = jnp.sum(jnp.where(own, s, 0.0), axis=0, keepdims=True)
        parts = []
        for b in range(CACHE_ROWS // LANES):
            tb = t[:, b * LANES:(b + 1) * LANES]
            parts.append(tb + jnp.where(low_half, pltpu.roll(tb, LANES - MEM_HEADS, 1),
                                        pltpu.roll(tb, MEM_HEADS, 1)))
        u = jnp.concatenate(parts, axis=1) * (MEM_HEADDIM ** -0.5)
        p = _softmax_rows(jnp.where(own, u, -jnp.inf))
        o8 = _dot(p.astype(BF16), v_ref[i].astype(BF16))
        for j in range(halves):
            for h in range(MEM_HEADS):
                r = j * MEM_HEADS + h
                o_ref[i, :, (h * halves + j) * LANES:(h * halves + j + 1) * LANES] = o8[r:r + 1, :]


def _cache_rows(c):
    depth, m = c.shape[:2]
    c = c.reshape(depth, m, N_MEM, MEM_HEADS, MEM_HEADDIM // LANES, LANES)
    return c.transpose(0, 1, 2, 4, 3, 5).reshape(depth, m, CACHE_ROWS, LANES)


def _sample_attn(q, k_rows, v_rows, layer, nb):
    m = q.shape[0]
    cache = pl.BlockSpec((None, nb, CACHE_ROWS, LANES), lambda i: (layer, i, 0, 0))
    return pl.pallas_call(
        _sattn_kernel,
        grid=(m // nb,),
        in_specs=[pl.BlockSpec((nb, 1, ATTN_W), lambda i: (i, 0, 0)), cache, cache],
        out_specs=pl.BlockSpec((nb, 1, ATTN_W), lambda i: (i, 0, 0)),
        out_shape=jax.ShapeDtypeStruct((m, 1, ATTN_W), F32),
        compiler_params=_params(("arbitrary",)),
        name="sample_attn",
    )(q.reshape(m, 1, ATTN_W), k_rows, v_rows)


def _post_kernel(x_ref, p_ref, acta_ref, xs_ref, ysd_ref, o_ref, vec_ref, wao_ref, wso_ref, wmo_ref, wo_ref,
                 out_ref, *, final_norm):
    y_a = _dot(acta_ref[...].astype(BF16), wao_ref[...])
    gz = ((ysd_ref[...] + _vec(vec_ref, _V_DSK, SSM_INNER) * xs_ref[...])
          * _silu(p_ref[:, _S_SZ:_S_SZ + SSM_INNER]))
    parts = []
    for g in range(SSM_GROUPS):
        gg = gz[:, g * GROUP_W:(g + 1) * GROUP_W]
        parts.append(gg * lax.rsqrt(jnp.mean(gg * gg, axis=-1, keepdims=True) + EPS))
    gn = jnp.concatenate(parts, axis=1) * _vec(vec_ref, _V_SNW, SSM_INNER)
    y_s = _dot(gn.astype(BF16), wso_ref[...])
    mz = p_ref[:, _S_M + ATTN_W:_S_M + 2 * ATTN_W]
    y_m = _dot((_silu(mz) * o_ref[...]).astype(BF16), wmo_ref[...])
    merged = (_sigmoid(p_ref[:, _S_G:_S_G + D_MODEL]) * y_a
              + _sigmoid(p_ref[:, _S_G + D_MODEL:_S_G + 2 * D_MODEL]) * y_s
              + _sigmoid(p_ref[:, _S_G + 2 * D_MODEL:_S_G + 3 * D_MODEL]) * y_m)
    xn = x_ref[...] + _dot(merged.astype(BF16), wo_ref[...])
    out_ref[...] = _rms(xn, _vec(vec_ref, _V_FNW, D_MODEL)) if final_norm else xn


def _sample_post(x, proj, acta, xs, ysd, o, vec, wao, wso, wmo, wo, layer, final_norm):
    m = x.shape[0]
    full = lambda a: _resident(a.shape, (0,) * a.ndim)
    acts = (x, proj, acta, xs, ysd, o)
    return pl.pallas_call(
        functools.partial(_post_kernel, final_norm=final_norm),
        grid=(1,),
        in_specs=([full(a) for a in acts]
                  + [_vec_spec(layer), _layer_mat((CONV_W, D_MODEL), layer), _layer_mat((SSM_INNER, D_MODEL), layer),
                     _layer_mat((ATTN_W, D_MODEL), layer), _layer_mat((D_MODEL, D_MODEL), layer)]),
        out_specs=_resident((m, D_MODEL), (0, 0)),
        out_shape=jax.ShapeDtypeStruct((m, D_MODEL), F32),
        compiler_params=_params(("arbitrary",)),
        name="sample_post",
    )(*acts, vec, wao, wso, wmo, wo)


def _pad_lanes(a):
    return jnp.pad(a, ((0, 0), (0, LANES - a.shape[1])))


def _prompt_tile(seq):
    for t in (512, 256, 128):
        if seq % t == 0:
            return t
    raise ValueError(f"sequence length {seq} must be a multiple of {SSD_CHUNK}")


def kernel(x_prompt, x_sample, mem_prompt, state_conv_a, state_conv_s, state_ssm, cache_mem_k, cache_mem_v,
           norm_w, w_in, conv_a_w, w_a_out, conv_s_w, conv_s_b, dt_bias, a_log, d_skip, ssm_norm_w,
           w_s_out, mem_norm_w, w_mem_kv, w_m_out, w_o, final_norm_w):
    depth = w_in.shape[0]
    bp, seq, _ = x_prompt.shape
    bs = x_sample.shape[0]
    assert x_sample.shape[1] == 1
    tile = _prompt_tile(seq)

    wdt = w_in[:, :, _OFF_DT:_OFF_M].astype(BF16)
    w_all = (w_in[:, :, :_OFF_DT].astype(BF16), jnp.repeat(wdt, SSM_HEADDIM, axis=2),
             w_in[:, :, _OFF_M:].astype(BF16))
    wdt = jnp.pad(wdt, ((0, 0), (0, 0), (0, LANES - SSM_HEADS)))
    wao, wso, wmo, wo, wkv = (a.astype(BF16) for a in (w_a_out, w_s_out, w_m_out, w_o, w_mem_kv))
    rep = lambda a: jnp.repeat(a, SSM_HEADDIM, axis=1)
    vec = jnp.concatenate([norm_w, mem_norm_w, conv_s_b, _pad_lanes(dt_bias), _pad_lanes(a_log), rep(d_skip),
                           ssm_norm_w, rep(dt_bias), rep(a_log),
                           jnp.broadcast_to(final_norm_w[None], (depth, D_MODEL))], axis=1)[:, None, :]

    x = x_prompt
    pa, ps, ph = [], [], []
    pk_rows, pv_rows, kt, vb = _mem_kv(mem_prompt, vec, wkv)
    for l in range(depth):
        last = l == depth - 1
        acc, ca = _prompt_am(x, vec, w_all, conv_a_w, wao, wmo, kt, vb, l, tile)
        x, cs, hs = _prompt_ssd(x, acc, vec, w_all, wdt, conv_s_w, wso, wo, l, tile, last)
        pa.append(ca)
        ps.append(cs)
        ph.append(hs.reshape(bp, SSM_HEADS, SSM_HEADDIM, SSM_STATE))
    y_prompt = x

    x = x_sample.reshape(bs, D_MODEL)
    state_all = state_ssm.reshape(depth, bs, SSM_HEADS * SSM_HEADDIM, SSM_STATE)
    k_rows, v_rows = _cache_rows(cache_mem_k), _cache_rows(cache_mem_v)
    sa, ss, s_ssm = [], [], None
    for l in range(depth):
        last = l == depth - 1
        proj = _sample_proj(x, vec, w_all, l)
        acta, rows, xs, nca, ncs = _sample_pre(
            proj, state_conv_a[l].reshape(bs, 2 * CONV_W), state_conv_s[l].reshape(bs, 3 * SSM_XBC),
            vec, conv_a_w, conv_s_w, l, 32)
        s_ssm, ysd = _sample_ssd(rows, state_all, l, s_ssm, SSD_SEQS_PER_STEP)
        o = _sample_attn(proj[:, _S_M:_S_M + ATTN_W], k_rows, v_rows, l, ATTN_SEQS_PER_STEP)
        x = _sample_post(x, proj, acta, xs, ysd.reshape(bs, SSM_INNER), o.reshape(bs, ATTN_W), vec,
                         wao, wso, wmo, wo, l, last)
        sa.append(nca.reshape(bs, 2, CONV_W))
        ss.append(ncs.reshape(bs, 3, SSM_XBC))
    y_sample = x.reshape(bs, 1, D_MODEL)

    return (y_prompt, y_sample, jnp.stack(pa), jnp.stack(ps), jnp.stack(ph), _cache_from_rows(pk_rows), _cache_from_rows(pv_rows),
            jnp.stack(sa), jnp.stack(ss), s_ssm.reshape(depth, bs, SSM_HEADS, SSM_HEADDIM, SSM_STATE))
```

```python
import functools

import jax
import jax.numpy as jnp
from jax import lax
from jax.experimental import pallas as pl
from jax.experimental.pallas import tpu as pltpu

F32 = jnp.float32
BF16 = jnp.bfloat16

D_MODEL = 1024
CONV_W = 1024
SSM_INNER = 2048
SSM_HEADDIM = 64
SSM_HEADS = 32
SSM_GROUPS = 4
SSM_STATE = 128
SSM_XBC = SSM_INNER + 2 * SSM_GROUPS * SSM_STATE
HEADS_PER_GROUP = SSM_HEADS // SSM_GROUPS
GROUP_W = SSM_INNER // SSM_GROUPS
N_PAIRS = SSM_HEADS // 2
N_MEM = 256
MEM_HEADS = 4
MEM_HEADDIM = 256
ATTN_W = MEM_HEADS * MEM_HEADDIM
EPS = 1e-6
LOG2E = 1.4426950408889634
SSD_CHUNK = 128
LANES = 128
SUBLANES = 8
CACHE_ROWS = N_MEM * ATTN_W // LANES
VMEM_LIMIT = 58 * 1024 * 1024
WBLK = 1024
SIDE_W = 256
SSD_SEQS_PER_STEP = 8
ATTN_SEQS_PER_STEP = 8
STRIPS = WBLK // LANES

_B_CB, _B_CC, _B_CH, _B_CZ = 0, 1, 2, 3
_B_SZ = 4
_B_SX = 6
_B_DT = 9
_B_Q, _B_MZ = 11, 12
_B_GA, _B_GS, _B_GM = 13, 14, 15
_N_BLK = 16
_S_A = 0
_S_SZ = _B_SZ * WBLK
_S_SX = _B_SX * WBLK
_S_DT = _B_DT * WBLK
_S_M = _B_Q * WBLK
_S_G = _B_GA * WBLK
_S_END = _N_BLK * WBLK
_OFF_DT = 4 * CONV_W + SSM_INNER + SSM_XBC
_OFF_M = _OFF_DT + SSM_HEADS
_R_DA = SSM_INNER
_R_B = 2 * SSM_INNER
_R_C = _R_B + SSM_GROUPS * SSM_STATE
_R_END = _R_C + SSM_GROUPS * SSM_STATE

_V_NW = 0
_V_MNW = _V_NW + D_MODEL
_V_CSB = _V_MNW + D_MODEL
_V_DTB = _V_CSB + SSM_XBC
_V_ALOG = _V_DTB + LANES
_V_DSK = _V_ALOG + LANES
_V_SNW = _V_DSK + SSM_INNER
_V_DTBE = _V_SNW + SSM_INNER
_V_ALOGE = _V_DTBE + SSM_INNER
_V_FNW = _V_ALOGE + SSM_INNER
_V_END = _V_FNW + D_MODEL


def _dot(a, b):
    return jnp.dot(a, b, preferred_element_type=F32)


def _dot_nt(a, b):
    return lax.dot_general(a, b, (((1,), (1,)), ((), ())), preferred_element_type=F32)


def _rms(x, w):
    return x * lax.rsqrt(jnp.mean(x * x, axis=-1, keepdims=True) + EPS) * w


def _sigmoid(x):
    return 0.5 * jnp.tanh(0.5 * x) + 0.5


def _silu(x):
    h = 0.5 * x
    return h * jnp.tanh(h) + h


def _softplus(x):
    return jnp.maximum(x, 0.0) + jnp.log1p(jnp.exp(-jnp.abs(x)))


def _softmax_rows(s):
    m = jnp.max(s, axis=-1, keepdims=True)
    e = jnp.exp(s - m)
    return e / jnp.sum(e, axis=-1, keepdims=True)


def _split3(x):
    hi = x.astype(BF16)
    r1 = x - hi.astype(F32)
    mid = r1.astype(BF16)
    lo = (r1 - mid.astype(F32)).astype(BF16)
    return hi, mid, lo


def _vec(vec_ref, off, width):
    return vec_ref[:, off:off + width]


def _resident(block, index):
    return pl.BlockSpec(block, lambda *_: index, pipeline_mode=pl.Buffered(1))


_SRC_FIRST = (0,)


def _wsrc(blk):
    src = max(i for i, first in enumerate(_SRC_FIRST) if blk >= first)
    return src, blk - _SRC_FIRST[src]


def _wblk(layer, blk):
    return _resident((None, D_MODEL, WBLK), (layer, 0, _wsrc(blk)[1]))


def _wargs(wsrc, blocks):
    return [wsrc[_wsrc(b)[0]] for b in blocks]


def _vec_spec(layer):
    return _resident((None, 1, _V_END), (layer, 0, 0))


def _layer_mat(shape, layer):
    return _resident((None,) + shape, (layer, 0, 0))


def _params(sem):
    return pltpu.CompilerParams(dimension_semantics=sem, vmem_limit_bytes=VMEM_LIMIT)


def _kv_kernel(mem_ref, vec_ref, w_ref, k_ref, v_ref, kt_ref, vb_ref):
    u = _rms(mem_ref[...], _vec(vec_ref, _V_MNW, D_MODEL)).astype(BF16)
    kv = _dot(u, w_ref[...])
    k = kv[:, :ATTN_W]
    v = kv[:, ATTN_W:]
    halves = MEM_HEADDIM // LANES
    for j in range(halves):
        for h in range(MEM_HEADS):
            rows = pl.ds(j * MEM_HEADS + h, N_MEM, stride=halves * MEM_HEADS)
            sl = slice((h * halves + j) * LANES, (h * halves + j + 1) * LANES)
            k_ref[rows, :] = k[:, sl]
            v_ref[rows, :] = v[:, sl]
    kt_ref[...] = (k.T * (MEM_HEADDIM ** -0.5)).astype(BF16)
    vb_ref[...] = v.astype(BF16)


def _mem_kv(mem, vec, w_kv):
    depth = w_kv.shape[0]
    bt = mem.shape[0]
    out = lambda r, c: pl.BlockSpec((None, None, r, c), lambda l, b: (l, b, 0, 0))
    per_layer = lambda r, c: pl.BlockSpec((None, r, c), lambda l, b: (l, 0, 0))
    return pl.pallas_call(
        _kv_kernel,
        grid=(depth, bt),
        in_specs=[pl.BlockSpec((None, N_MEM, D_MODEL), lambda l, b: (b, 0, 0)), per_layer(1, _V_END),
                  per_layer(D_MODEL, 2 * ATTN_W)],
        out_specs=[out(CACHE_ROWS, LANES), out(CACHE_ROWS, LANES), out(ATTN_W, N_MEM), out(N_MEM, ATTN_W)],
        out_shape=[jax.ShapeDtypeStruct((depth, bt, CACHE_ROWS, LANES), F32),
                   jax.ShapeDtypeStruct((depth, bt, CACHE_ROWS, LANES), F32),
                   jax.ShapeDtypeStruct((depth, bt, ATTN_W, N_MEM), BF16),
                   jax.ShapeDtypeStruct((depth, bt, N_MEM, ATTN_W), BF16)],
        compiler_params=_params(("arbitrary", "arbitrary")),
        name="prompt_kv",
    )(mem, vec, w_kv)


def _cache_from_rows(c):
    depth, m = c.shape[:2]
    c = c.reshape(depth, m, N_MEM, MEM_HEADDIM // LANES, MEM_HEADS, LANES)
    return c.transpose(0, 1, 2, 4, 3, 5).reshape(depth, m, N_MEM, MEM_HEADS, MEM_HEADDIM)


def _am_kernel(x_ref, vec_ref, wcb_ref, wcc_ref, wch_ref, wcz_ref, wq_ref, wmz_ref, wga_ref, wgm_ref,
               caw_ref, wao_ref, wmo_ref, kt_ref, vb_ref,
               acc_ref, ca_ref, vbuf):
    c = pl.program_id(1)
    t = x_ref.shape[0]
    pad = SUBLANES

    @pl.when(c == 0)
    def _():
        vbuf[:, 0:pad, :] = jnp.zeros((STRIPS, pad, LANES), F32)

    ub = _rms(x_ref[...], _vec(vec_ref, _V_NW, D_MODEL)).astype(BF16)

    v = _dot(ub, wcc_ref[...]) * _dot(ub, wch_ref[...])
    gate = _silu(_dot(ub, wcz_ref[...])) * _dot(ub, wcb_ref[...])
    caw = caw_ref[...]
    acts = []
    for j in range(STRIPS):
        sl = slice(j * LANES, (j + 1) * LANES)
        vj = v[:, sl]
        vbuf[j, pad:pad + t, :] = vj
        conv = (caw[0:1, sl] * vbuf[j, pad - 2:pad - 2 + t, :] + caw[1:2, sl] * vbuf[j, pad - 1:pad - 1 + t, :]
                + caw[2:3, sl] * vj)
        acts.append((gate[:, sl] * conv).astype(BF16))
        ca_ref[:, sl] = vbuf[j, pad + t - 2:pad + t, :]
        vbuf[j, 0:pad, :] = vbuf[j, t:t + pad, :]
    y_a = _dot(jnp.concatenate(acts, axis=1), wao_ref[...])

    q = _dot(ub, wq_ref[...])
    mz = _dot(ub, wmz_ref[...])
    acts = []
    for h in range(MEM_HEADS):
        sl = slice(h * MEM_HEADDIM, (h + 1) * MEM_HEADDIM)
        p = _softmax_rows(_dot(q[:, sl].astype(BF16), kt_ref[sl, :]))
        o = _dot(p.astype(BF16), vb_ref[:, sl])
        acts.append((_silu(mz[:, sl]) * o).astype(BF16))
    y_m = _dot(jnp.concatenate(acts, axis=1), wmo_ref[...])

    acc_ref[...] = _sigmoid(_dot(ub, wga_ref[...])) * y_a + _sigmoid(_dot(ub, wgm_ref[...])) * y_m


def _prompt_am(x, vec, w_all, caw, wao, wmo, kt, vb, layer, tile):
    bt, seq, _ = x.shape
    nc = seq // tile
    tok = lambda w: pl.BlockSpec((None, tile, w), lambda b, c: (b, c, 0))
    per_b = lambda r, w: pl.BlockSpec((None, r, w), lambda b, c: (b, 0, 0))
    mem_kv = lambda r, w: pl.BlockSpec((None, None, r, w), lambda b, c: (layer, b, 0, 0))
    blocks = (_B_CB, _B_CC, _B_CH, _B_CZ, _B_Q, _B_MZ, _B_GA, _B_GM)
    return pl.pallas_call(
        _am_kernel,
        grid=(bt, nc),
        in_specs=([tok(D_MODEL), _vec_spec(layer)] + [_wblk(layer, b) for b in blocks]
                  + [_layer_mat((3, CONV_W), layer), _layer_mat((CONV_W, D_MODEL), layer),
                     _layer_mat((ATTN_W, D_MODEL), layer), mem_kv(ATTN_W, N_MEM), mem_kv(N_MEM, ATTN_W)]),
        out_specs=[tok(D_MODEL), per_b(2, CONV_W)],
        out_shape=[jax.ShapeDtypeStruct((bt, seq, D_MODEL), F32),
                   jax.ShapeDtypeStruct((bt, 2, CONV_W), F32)],
        scratch_shapes=[pltpu.VMEM((STRIPS, tile + SUBLANES, LANES), F32)],
        compiler_params=_params(("arbitrary", "arbitrary")),
        name="prompt_am",
    )(x, vec, *_wargs(w_all, blocks), caw, wao, wmo, kt, vb)


def _cumsum_rows(da):
    q = da.shape[0]
    r = lax.broadcasted_iota(jnp.int32, (q, q), 0)
    c = lax.broadcasted_iota(jnp.int32, (q, q), 1)
    tri = jnp.where(r >= c, 1.0, 0.0).astype(BF16)
    hi, mid, lo = _split3(da)
    return _dot(tri, hi) + _dot(tri, mid) + _dot(tri, lo)


def _ssd_kernel(x_ref, acc_ref, vec_ref, wsz0_ref, wsz1_ref, wsx0_ref, wsx1_ref, wsx2_ref, wgs_ref, wdt_ref,
                csw_ref, wso_ref, wo_ref,
                out_ref, cs_ref, ssm_ref,
                xbuf, tails, xbc_scr, dt_scr, ht_scr, *, final_norm):
    c = pl.program_id(1)
    t = x_ref.shape[0]
    q = SSD_CHUNK
    pad = SUBLANES

    @pl.when(c == 0)
    def _():
        tails[...] = jnp.zeros(tails.shape, F32)
        ht_scr[...] = jnp.zeros(ht_scr.shape, F32)

    ub = _rms(x_ref[...], _vec(vec_ref, _V_NW, D_MODEL)).astype(BF16)

    csw = csw_ref[...]
    for b, w_ref in enumerate((wsx0_ref, wsx1_ref, wsx2_ref)):
        sx = _dot(ub, w_ref[...])
        for j in range(STRIPS):
            s = b * STRIPS + j
            sl = slice(s * LANES, (s + 1) * LANES)
            sxj = sx[:, j * LANES:(j + 1) * LANES]
            xbuf[j, 0:pad, :] = tails[s]
            xbuf[j, pad:pad + t, :] = sxj
            yc = (csw[0:1, sl] * xbuf[j, pad - 3:pad - 3 + t, :] + csw[1:2, sl] * xbuf[j, pad - 2:pad - 2 + t, :]
                  + csw[2:3, sl] * xbuf[j, pad - 1:pad - 1 + t, :] + csw[3:4, sl] * sxj
                  + vec_ref[:, _V_CSB + s * LANES:_V_CSB + (s + 1) * LANES])
            xbc_scr[:, sl] = _silu(yc)
            cs_ref[:, sl] = xbuf[j, pad + t - 3:pad + t, :]
            tails[s] = xbuf[j, t:t + pad, :]
    dt_scr[...] = _softplus(_dot(ub, wdt_ref[...]) + _vec(vec_ref, _V_DTB, LANES))
    a_row = -jnp.exp(_vec(vec_ref, _V_ALOG, LANES))

    rr = lax.broadcasted_iota(jnp.int32, (q, q), 0)
    cc = lax.broadcasted_iota(jnp.int32, (q, q), 1)
    tril = rr >= cc
    left = lax.broadcasted_iota(jnp.int32, (1, LANES), 1) < SSM_HEADDIM

    def chunk(j, after_pair):
        rows = slice(j * q, (j + 1) * q)
        dtj = dt_scr[rows, :]
        cs = _cumsum_rows(dtj * a_row) * LOG2E
        cs_t = cs.T
        dt_t = dtj.T
        w_t = dt_t * jnp.exp2(cs_t[:, q - 1:q] - cs_t)
        etot = jnp.exp2(cs[q - 1:q, :])

        scores, bm_t, cms = [], [], []
        for g in range(SSM_GROUPS):
            bg = xbc_scr[rows, SSM_INNER + g * SSM_STATE:SSM_INNER + (g + 1) * SSM_STATE]
            cg = xbc_scr[rows, SSM_INNER + (SSM_GROUPS + g) * SSM_STATE:
                         SSM_INNER + (SSM_GROUPS + g + 1) * SSM_STATE]
            scores.append(_dot_nt(cg.astype(BF16), bg.astype(BF16)))
            bm_t.append(bg.T)
            cms.append(cg)

        for pair in range(N_PAIRS):
            g = (2 * pair) // HEADS_PER_GROUP
            lanes = slice(pair * LANES, (pair + 1) * LANES)
            xs_p = xbc_scr[rows, lanes]
            ht = ht_scr[pair]
            xsb = xs_p.astype(BF16)
            rhs = jnp.concatenate([xsb, ht.astype(BF16)], axis=0)
            ys, upds = [], []
            for h in (2 * pair, 2 * pair + 1):
                col = jnp.broadcast_to(cs[:, h:h + 1], (q, q))
                row = jnp.broadcast_to(cs_t[h:h + 1, :], (q, q))
                lmat = jnp.exp2(jnp.where(tril, col - row, -jnp.inf))
                lhs = jnp.concatenate([(scores[g] * lmat * dt_t[h:h + 1, :]).astype(BF16),
                                       (cms[g] * jnp.exp2(col)).astype(BF16)], axis=1)
                ys.append(_dot(lhs, rhs))
                upds.append(_dot((bm_t[g] * w_t[h:h + 1, :]).astype(BF16), xsb))
            xbc_scr[rows, lanes] = (jnp.where(left, ys[0], ys[1])
                                    + vec_ref[:, _V_DSK + pair * LANES:_V_DSK + (pair + 1) * LANES] * xs_p)
            dec = jnp.where(left, jnp.broadcast_to(etot[:, 2 * pair:2 * pair + 1], (1, LANES)),
                            jnp.broadcast_to(etot[:, 2 * pair + 1:2 * pair + 2], (1, LANES)))
            ht_scr[pair] = ht * dec + jnp.where(left, upds[0], upds[1])
            after_pair(j * N_PAIRS + pair)

    jobs = [(w_ref, k, act) for w_ref, act in ((wsz0_ref, _silu), (wsz1_ref, _silu), (wgs_ref, _sigmoid))
            for k in range(WBLK // SIDE_W)]
    n_slots = (t // q) * N_PAIRS
    job_at = {((2 * i + 1) * n_slots) // (2 * len(jobs)): i for i in range(len(jobs))}
    assert len(job_at) == len(jobs)
    side_vals = []

    def after_pair(slot):
        if slot in job_at:
            w_ref, k, act = jobs[job_at[slot]]
            side_vals.append(act(_dot(ub, w_ref[:, k * SIDE_W:(k + 1) * SIDE_W])))

    for j in range(t // q):
        chunk(j, after_pair)
    per_blk = WBLK // SIDE_W
    zgate = [jnp.concatenate(side_vals[b * per_blk:(b + 1) * per_blk], axis=1) for b in range(2)]
    g_s = jnp.concatenate(side_vals[2 * per_blk:], axis=1)

    @pl.when(c == pl.num_programs(1) - 1)
    def _():
        for pair in range(N_PAIRS):
            ssm_ref[pair * LANES:(pair + 1) * LANES, :] = ht_scr[pair].T

    parts = []
    for b in range(SSM_INNER // WBLK):
        gz = xbc_scr[:, b * WBLK:(b + 1) * WBLK] * zgate[b]
        for g in range(WBLK // GROUP_W):
            gg = gz[:, g * GROUP_W:(g + 1) * GROUP_W]
            off = _V_SNW + b * WBLK + g * GROUP_W
            parts.append((gg * lax.rsqrt(jnp.mean(gg * gg, axis=-1, keepdims=True) + EPS)
                          * vec_ref[:, off:off + GROUP_W]).astype(BF16))
    y_s = _dot(jnp.concatenate(parts, axis=1), wso_ref[...])
    merged = acc_ref[...] + g_s * y_s
    xn = x_ref[...] + _dot(merged.astype(BF16), wo_ref[...])
    out_ref[...] = _rms(xn, _vec(vec_ref, _V_FNW, D_MODEL)) if final_norm else xn


def _prompt_ssd(x, acc, vec, w_all, wdt, csw, wso, wo, layer, tile, final_norm):
    bt, seq, _ = x.shape
    nc = seq // tile
    tok = lambda w: pl.BlockSpec((None, tile, w), lambda b, c: (b, c, 0))
    per_b = lambda r, w: pl.BlockSpec((None, r, w), lambda b, c: (b, 0, 0))
    blocks = (_B_SZ, _B_SZ + 1, _B_SX, _B_SX + 1, _B_SX + 2, _B_GS)
    return pl.pallas_call(
        functools.partial(_ssd_kernel, final_norm=final_norm),
        grid=(bt, nc),
        in_specs=([tok(D_MODEL), tok(D_MODEL), _vec_spec(layer)] + [_wblk(layer, b) for b in blocks]
                  + [_layer_mat((D_MODEL, LANES), layer), _layer_mat((4, SSM_XBC), layer),
                     _layer_mat((SSM_INNER, D_MODEL), layer), _layer_mat((D_MODEL, D_MODEL), layer)]),
        out_specs=[tok(D_MODEL), per_b(3, SSM_XBC), per_b(SSM_HEADS * SSM_HEADDIM, SSM_STATE)],
        out_shape=[jax.ShapeDtypeStruct((bt, seq, D_MODEL), F32),
                   jax.ShapeDtypeStruct((bt, 3, SSM_XBC), F32),
                   jax.ShapeDtypeStruct((bt, SSM_HEADS * SSM_HEADDIM, SSM_STATE), F32)],
        scratch_shapes=[pltpu.VMEM((STRIPS, tile + SUBLANES, LANES), F32),
                        pltpu.VMEM((SSM_XBC // LANES, SUBLANES, LANES), F32),
                        pltpu.VMEM((tile, SSM_XBC), F32),
                        pltpu.VMEM((tile, LANES), F32),
                        pltpu.VMEM((N_PAIRS, SSM_STATE, LANES), F32)],
        compiler_params=_params(("arbitrary", "arbitrary")),
        name="prompt_ssd",
    )(x, acc, vec, *_wargs(w_all, blocks), wdt, csw, wso, wo)


def _proj_kernel(x_ref, vec_ref, *refs):
    w_refs, o_ref = refs[:-1], refs[-1]
    j = pl.program_id(0)
    ub = _rms(x_ref[...], _vec(vec_ref, _V_NW, D_MODEL)).astype(BF16)
    bounds = _SRC_FIRST + (_N_BLK,)
    for src, w_ref in enumerate(w_refs):
        @pl.when((j >= bounds[src]) & (j < bounds[src + 1]))
        def _(w_ref=w_ref):
            o_ref[...] = _dot(ub, w_ref[...])


def _sample_proj(x, vec, w_all, layer):
    m = x.shape[0]
    bounds = _SRC_FIRST + (_N_BLK,)

    def wspec(src):
        first, count = bounds[src], bounds[src + 1] - bounds[src]
        return pl.BlockSpec((None, D_MODEL, WBLK), lambda j: (layer, 0, jnp.clip(j - first, 0, count - 1)))

    return pl.pallas_call(
        _proj_kernel,
        grid=(_N_BLK,),
        in_specs=[_resident((m, D_MODEL), (0, 0)), _vec_spec(layer)] + [wspec(s) for s in range(len(w_all))],
        out_specs=pl.BlockSpec((m, WBLK), lambda j: (0, j)),
        out_shape=jax.ShapeDtypeStruct((m, _S_END), F32),
        compiler_params=_params(("arbitrary",)),
        name="sample_proj",
    )(x, vec, *w_all)


def _pre_kernel(p_ref, sca_ref, scs_ref, vec_ref, caw_ref, csw_ref,
                acta_ref, row_ref, xs_ref, nca_ref, ncs_ref):
    cb = p_ref[:, _S_A:_S_A + CONV_W]
    v = p_ref[:, _S_A + CONV_W:_S_A + 2 * CONV_W] * p_ref[:, _S_A + 2 * CONV_W:_S_A + 3 * CONV_W]
    cz = p_ref[:, _S_A + 3 * CONV_W:_S_A + 4 * CONV_W]
    caw = caw_ref[...]
    b0 = sca_ref[:, 0:CONV_W]
    b1 = sca_ref[:, CONV_W:2 * CONV_W]
    conv = caw[0:1] * b0 + caw[1:2] * b1 + caw[2:3] * v
    acta_ref[...] = _silu(cz) * cb * conv
    nca_ref[:, 0:CONV_W] = b1
    nca_ref[:, CONV_W:2 * CONV_W] = v

    sxbc = p_ref[:, _S_SX:_S_SX + SSM_XBC]
    csw = csw_ref[...]
    s0 = scs_ref[0]
    s1 = scs_ref[1]
    s2 = scs_ref[2]
    xbc = _silu(csw[0:1] * s0 + csw[1:2] * s1 + csw[2:3] * s2 + csw[3:4] * sxbc + _vec(vec_ref, _V_CSB, SSM_XBC))
    ncs_ref[0] = s1
    ncs_ref[1] = s2
    ncs_ref[2] = sxbc
    xs = xbc[:, :SSM_INNER]
    dt = _softplus(p_ref[:, _S_DT:_S_DT + SSM_INNER] + _vec(vec_ref, _V_DTBE, SSM_INNER))
    xs_ref[...] = xs
    row_ref[:, 0:_R_DA] = xs * dt
    row_ref[:, _R_DA:_R_B] = jnp.exp(dt * -jnp.exp(_vec(vec_ref, _V_ALOGE, SSM_INNER)))
    row_ref[:, _R_B:_R_END] = xbc[:, SSM_INNER:]


def _sample_pre(proj, sca, scs_all, vec, caw, csw, layer, rb):
    m = proj.shape[0]
    taps = scs_all.shape[1]
    rows = lambda w: pl.BlockSpec((rb, w), lambda i: (i, 0))
    return pl.pallas_call(
        _pre_kernel,
        grid=(m // rb,),
        in_specs=[rows(_S_END), rows(2 * CONV_W),
                  pl.BlockSpec((None, taps, rb, SSM_XBC), lambda i: (layer, 0, i, 0)), _vec_spec(layer),
                  _layer_mat((3, CONV_W), layer), _layer_mat((4, SSM_XBC), layer)],
        out_specs=[rows(CONV_W), rows(_R_END), rows(SSM_INNER), rows(2 * CONV_W),
                   pl.BlockSpec((taps, rb, SSM_XBC), lambda i: (0, i, 0))],
        out_shape=[jax.ShapeDtypeStruct((m, CONV_W), F32),
                   jax.ShapeDtypeStruct((m, _R_END), F32),
                   jax.ShapeDtypeStruct((m, SSM_INNER), F32),
                   jax.ShapeDtypeStruct((m, 2 * CONV_W), F32),
                   jax.ShapeDtypeStruct((taps, m, SSM_XBC), F32)],
        compiler_params=_params(("arbitrary",)),
        name="sample_pre",
    )(proj, sca, scs_all, vec, caw, csw)


def _sssd_kernel(row_ref, h0_ref, *rest):
    hn_ref, y_ref = rest[-2:]
    nb = row_ref.shape[0]
    sub_x = lax.broadcasted_iota(jnp.int32, (LANES, GROUP_W), 0)
    sub_r = lax.broadcasted_iota(jnp.int32, (LANES, 2 * SSM_STATE), 0)
    eye = (lax.broadcasted_iota(jnp.int32, (LANES, SSM_STATE), 0)
           == lax.broadcasted_iota(jnp.int32, (LANES, SSM_STATE), 1))
    zero = jnp.zeros((1, SSM_STATE), F32)
    for i in range(nb):
        row = row_ref[i]
        xparts = []
        rk = jnp.zeros((LANES, 2 * SSM_STATE), F32)
        for g in range(SSM_GROUPS):
            xdt = row[:, g * GROUP_W:(g + 1) * GROUP_W]
            x_hi = xdt.astype(BF16).astype(F32)
            x_lo = xdt - x_hi
            da = row[:, _R_DA + g * GROUP_W:_R_DA + (g + 1) * GROUP_W]
            a_hi = da.astype(BF16).astype(F32)
            a_mid = (da - a_hi).astype(BF16).astype(F32)
            a_lo = da - a_hi - a_mid
            k = sub_x - g * SUBLANES
            xparts.append(jnp.where((k == 0) | (k == 2), x_hi,
                                    jnp.where((k == 1) | (k == 3), x_lo,
                                              jnp.where(k == 4, a_hi,
                                                        jnp.where(k == 5, a_mid,
                                                                  jnp.where(k == 6, a_lo, 0.0))))))
            bg = row[:, _R_B + g * SSM_STATE:_R_B + (g + 1) * SSM_STATE]
            b_hi = bg.astype(BF16).astype(F32)
            b_lo = bg - b_hi
            kr = sub_r - g * SUBLANES
            rk = jnp.where((kr == 0) | (kr == 1), jnp.concatenate([b_hi, zero], axis=1),
                           jnp.where((kr == 2) | (kr == 3), jnp.concatenate([b_lo, zero], axis=1),
                                     jnp.where((kr >= 4) & (kr < 7), jnp.concatenate([zero, zero + 1.0], axis=1),
                                               rk)))
        xk = jnp.concatenate(xparts, axis=1)
        bc = _dot(xk.T.astype(BF16), rk.astype(BF16))
        hnew = bc[:, SSM_STATE:] * h0_ref[i] + bc[:, :SSM_STATE]
        hn_ref[i] = hnew
        hb = hnew.astype(BF16)
        ys = []
        for g in range(SSM_GROUPS):
            cg = row[:, _R_C + g * SSM_STATE:_R_C + (g + 1) * SSM_STATE]
            c_cols = jnp.broadcast_to(cg, (SSM_STATE, SSM_STATE)).T.astype(BF16)
            rep = _dot(hb[g * GROUP_W:(g + 1) * GROUP_W, :], c_cols)
            for k in range(GROUP_W // LANES):
                ys.append(jnp.sum(jnp.where(eye, rep[k * LANES:(k + 1) * LANES, :], 0.0), axis=0, keepdims=True))
        y_ref[i] = jnp.concatenate(ys, axis=1)


def _sample_ssd(rows, state_all, layer, stacked, nb):
    depth, m, hp, _ = state_all.shape
    state_spec = pl.BlockSpec((None, nb, hp, SSM_STATE), lambda i: (layer, i, 0, 0))
    in_specs = [pl.BlockSpec((nb, 1, _R_END), lambda i: (i, 0, 0)), state_spec]
    args = [rows.reshape(m, 1, _R_END), state_all]
    aliases = {}
    if stacked is not None:
        in_specs.append(pl.BlockSpec(memory_space=pl.ANY))
        args.append(stacked)
        aliases = {2: 0}
    return pl.pallas_call(
        _sssd_kernel,
        grid=(m // nb,),
        in_specs=in_specs,
        out_specs=[state_spec, pl.BlockSpec((nb, 1, hp), lambda i: (i, 0, 0))],
        out_shape=[jax.ShapeDtypeStruct((depth, m, hp, SSM_STATE), F32),
                   jax.ShapeDtypeStruct((m, 1, hp), F32)],
        input_output_aliases=aliases,
        compiler_params=_params(("arbitrary",)),
        name="sample_ssd",
    )(*args)


def _sattn_kernel(q_ref, k_ref, v_ref, o_ref):
    nb = q_ref.shape[0]
    halves = MEM_HEADDIM // LANES
    rows = halves * MEM_HEADS
    row = lax.broadcasted_iota(jnp.int32, (rows, CACHE_ROWS), 0)
    col = lax.broadcasted_iota(jnp.int32, (rows, CACHE_ROWS), 1)
    own = (col & (rows - 1)) == row
    lane = lax.broadcasted_iota(jnp.int32, (1, LANES), 1)
    low_half = (lane & MEM_HEADS) == 0
    for i in range(nb):
        q = q_ref[i]
        q8 = jnp.concatenate([q[:, (h * halves + j) * LANES:(h * halves + j + 1) * LANES]
                              for j in range(halves) for h in range(MEM_HEADS)], axis=0)
        s = _dot_nt(q8.astype(BF16), k_ref[i].astype(BF16))
        t = jnp.sum(jnp.where(own, s, 0.0), axis=0, keepdims=True)
        parts = []
        for b in range(CACHE_ROWS // LANES):
            tb = t[:, b * LANES:(b + 1) * LANES]
            parts.append(tb + jnp.where(low_half, pltpu.roll(tb, LANES - MEM_HEADS, 1),
                                        pltpu.roll(tb, MEM_HEADS, 1)))
        u = jnp.concatenate(parts, axis=1) * (MEM_HEADDIM ** -0.5)
        p = _softmax_rows(jnp.where(own, u, -jnp.inf))
        o8 = _dot(p.astype(BF16), v_ref[i].astype(BF16))
        for j in range(halves):
            for h in range(MEM_HEADS):
                r = j * MEM_HEADS + h
                o_ref[i, :, (h * halves + j) * LANES:(h * halves + j + 1) * LANES] = o8[r:r + 1, :]


def _cache_rows(c):
    depth, m = c.shape[:2]
    c = c.reshape(depth, m, N_MEM, MEM_HEADS, MEM_HEADDIM // LANES, LANES)
    return c.transpose(0, 1, 2, 4, 3, 5).reshape(depth, m, CACHE_ROWS, LANES)


def _sample_attn(q, k_rows, v_rows, layer, nb):
    m = q.shape[0]
    cache = pl.BlockSpec((None, nb, CACHE_ROWS, LANES), lambda i: (layer, i, 0, 0))
    return pl.pallas_call(
        _sattn_kernel,
        grid=(m // nb,),
        in_specs=[pl.BlockSpec((nb, 1, ATTN_W), lambda i: (i, 0, 0)), cache, cache],
        out_specs=pl.BlockSpec((nb, 1, ATTN_W), lambda i: (i, 0, 0)),
        out_shape=jax.ShapeDtypeStruct((m, 1, ATTN_W), F32),
        compiler_params=_params(("arbitrary",)),
        name="sample_attn",
    )(q.reshape(m, 1, ATTN_W), k_rows, v_rows)


def _post_kernel(x_ref, p_ref, acta_ref, xs_ref, ysd_ref, o_ref, vec_ref, wao_ref, wso_ref, wmo_ref, wo_ref,
                 out_ref, *, final_norm):
    y_a = _dot(acta_ref[...].astype(BF16), wao_ref[...])
    gz = ((ysd_ref[...] + _vec(vec_ref, _V_DSK, SSM_INNER) * xs_ref[...])
          * _silu(p_ref[:, _S_SZ:_S_SZ + SSM_INNER]))
    parts = []
    for g in range(SSM_GROUPS):
        gg = gz[:, g * GROUP_W:(g + 1) * GROUP_W]
        parts.append(gg * lax.rsqrt(jnp.mean(gg * gg, axis=-1, keepdims=True) + EPS))
    gn = jnp.concatenate(parts, axis=1) * _vec(vec_ref, _V_SNW, SSM_INNER)
    y_s = _dot(gn.astype(BF16), wso_ref[...])
    mz = p_ref[:, _S_M + ATTN_W:_S_M + 2 * ATTN_W]
    y_m = _dot((_silu(mz) * o_ref[...]).astype(BF16), wmo_ref[...])
    merged = (_sigmoid(p_ref[:, _S_G:_S_G + D_MODEL]) * y_a
              + _sigmoid(p_ref[:, _S_G + D_MODEL:_S_G + 2 * D_MODEL]) * y_s
              + _sigmoid(p_ref[:, _S_G + 2 * D_MODEL:_S_G + 3 * D_MODEL]) * y_m)
    xn = x_ref[...] + _dot(merged.astype(BF16), wo_ref[...])
    out_ref[...] = _rms(xn, _vec(vec_ref, _V_FNW, D_MODEL)) if final_norm else xn


def _sample_post(x, proj, acta, xs, ysd, o, vec, wao, wso, wmo, wo, layer, final_norm):
    m = x.shape[0]
    full = lambda a: _resident(a.shape, (0,) * a.ndim)
    acts = (x, proj, acta, xs, ysd, o)
    return pl.pallas_call(
        functools.partial(_post_kernel, final_norm=final_norm),
        grid=(1,),
        in_specs=([full(a) for a in acts]
                  + [_vec_spec(layer), _layer_mat((CONV_W, D_MODEL), layer), _layer_mat((SSM_INNER, D_MODEL), layer),
                     _layer_mat((ATTN_W, D_MODEL), layer), _layer_mat((D_MODEL, D_MODEL), layer)]),
        out_specs=_resident((m, D_MODEL), (0, 0)),
        out_shape=jax.ShapeDtypeStruct((m, D_MODEL), F32),
        compiler_params=_params(("arbitrary",)),
        name="sample_post",
    )(*acts, vec, wao, wso, wmo, wo)


_TAIL_FIRST_A = _OFF_DT // WBLK
_TAIL_FIRST_B = _OFF_DT // LANES


def _regroup_kernel(a_ref, b_ref, last_ref, o_ref):
    j = pl.program_id(1)

    @pl.when(j < _B_DT)
    def _():
        o_ref[...] = a_ref[...].astype(BF16)

    @pl.when((j >= _B_DT) & (j < _B_Q))
    def _():
        r = lax.broadcasted_iota(jnp.int32, (LANES, WBLK), 0)
        c = lax.broadcasted_iota(jnp.int32, (LANES, WBLK), 1)
        head = (j - _B_DT) * (WBLK // SSM_HEADDIM) + c // SSM_HEADDIM
        o_ref[...] = _dot(b_ref[...].astype(BF16), jnp.where(r == head, 1.0, 0.0).astype(BF16)).astype(BF16)

    @pl.when(j >= _B_Q)
    def _():
        nxt = jnp.where(j == _N_BLK - 1, last_ref[...], b_ref[...])
        lane = lax.broadcasted_iota(jnp.int32, (1, LANES), 1)
        nxt = jnp.where(lane < SSM_HEADS, nxt, 0.0)
        ab = jnp.concatenate([a_ref[...], nxt], axis=1).astype(BF16)
        r = lax.broadcasted_iota(jnp.int32, (WBLK + LANES, WBLK), 0)
        c = lax.broadcasted_iota(jnp.int32, (WBLK + LANES, WBLK), 1)
        o_ref[...] = _dot(ab, jnp.where(r == c + SSM_HEADS, 1.0, 0.0).astype(BF16)).astype(BF16)


def _regroup_w_in(w_in):
    depth, _, n_in = w_in.shape
    last = jnp.pad(w_in[:, :, n_in - SSM_HEADS:], ((0, 0), (0, 0), (0, LANES - SSM_HEADS)))
    a_idx = lambda j: jnp.where(j < _B_DT, j, jnp.clip(j - _B_Q + _TAIL_FIRST_A, _TAIL_FIRST_A, n_in // WBLK - 1))
    b_idx = lambda j: jnp.where(j < _B_Q, _TAIL_FIRST_B,
                                jnp.minimum(_TAIL_FIRST_B + (WBLK // LANES) * (j - _B_Q + 1), n_in // LANES - 1))
    return pl.pallas_call(
        _regroup_kernel,
        grid=(depth, _N_BLK),
        in_specs=[pl.BlockSpec((None, D_MODEL, WBLK), lambda l, j: (l, 0, a_idx(j))),
                  pl.BlockSpec((None, D_MODEL, LANES), lambda l, j: (l, 0, b_idx(j))),
                  pl.BlockSpec((None, D_MODEL, LANES), lambda l, j: (l, 0, 0))],
        out_specs=pl.BlockSpec((None, D_MODEL, WBLK), lambda l, j: (l, 0, j)),
        out_shape=jax.ShapeDtypeStruct((depth, D_MODEL, _S_END), BF16),
        compiler_params=_params(("arbitrary", "arbitrary")),
        name="regroup_w_in",
    )(w_in, w_in, last)


def _pad_lanes(a):
    return jnp.pad(a, ((0, 0), (0, LANES - a.shape[1])))


def _prompt_tile(seq):
    for t in (512, 256, 128):
        if seq % t == 0:
            return t
    raise ValueError(f"sequence length {seq} must be a multiple of {SSD_CHUNK}")


def kernel(x_prompt, x_sample, mem_prompt, state_conv_a, state_conv_s, state_ssm, cache_mem_k, cache_mem_v,
           norm_w, w_in, conv_a_w, w_a_out, conv_s_w, conv_s_b, dt_bias, a_log, d_skip, ssm_norm_w,
           w_s_out, mem_norm_w, w_mem_kv, w_m_out, w_o, final_norm_w):
    depth = w_in.shape[0]
    bp, seq, _ = x_prompt.shape
    bs = x_sample.shape[0]
    assert x_sample.shape[1] == 1
    tile = _prompt_tile(seq)

    w_all = (_regroup_w_in(w_in),)
    wdt = jnp.pad(w_in[:, :, _OFF_DT:_OFF_M].astype(BF16), ((0, 0), (0, 0), (0, LANES - SSM_HEADS)))
    wao, wso, wmo, wo, wkv = (a.astype(BF16) for a in (w_a_out, w_s_out, w_m_out, w_o, w_mem_kv))
    rep = lambda a: jnp.repeat(a, SSM_HEADDIM, axis=1)
    vec = jnp.concatenate([norm_w, mem_norm_w, conv_s_b, _pad_lanes(dt_bias), _pad_lanes(a_log), rep(d_skip),
                           ssm_norm_w, rep(dt_bias), rep(a_log),
                           jnp.broadcast_to(final_norm_w[None], (depth, D_MODEL))], axis=1)[:, None, :]

    x = x_prompt
    pa, ps, ph = [], [], []
    pk_rows, pv_rows, kt, vb = _mem_kv(mem_prompt, vec, wkv)
    for l in range(depth):
        last = l == depth - 1
        acc, ca = _prompt_am(x, vec, w_all, conv_a_w, wao, wmo, kt, vb, l, tile)
        x, cs, hs = _prompt_ssd(x, acc, vec, w_all, wdt, conv_s_w, wso, wo, l, tile, last)
        pa.append(ca)
        ps.append(cs)
        ph.append(hs.reshape(bp, SSM_HEADS, SSM_HEADDIM, SSM_STATE))
    y_prompt = x

    x = x_sample.reshape(bs, D_MODEL)
    state_all = state_ssm.reshape(depth, bs, SSM_HEADS * SSM_HEADDIM, SSM_STATE)
    k_rows, v_rows = _cache_rows(cache_mem_k), _cache_rows(cache_mem_v)
    scs_all = state_conv_s.transpose(0, 2, 1, 3)
    sa, ss, s_ssm = [], [], None
    for l in range(depth):
        last = l == depth - 1
        proj = _sample_proj(x, vec, w_all, l)
        acta, rows, xs, nca, ncs = _sample_pre(
            proj, state_conv_a[l].reshape(bs, 2 * CONV_W), scs_all, vec, conv_a_w, conv_s_w, l, 32)
        s_ssm, ysd = _sample_ssd(rows, state_all, l, s_ssm, SSD_SEQS_PER_STEP)
        o = _sample_attn(proj[:, _S_M:_S_M + ATTN_W], k_rows, v_rows, l, ATTN_SEQS_PER_STEP)
        x = _sample_post(x, proj, acta, xs, ysd.reshape(bs, SSM_INNER), o.reshape(bs, ATTN_W), vec,
                         wao, wso, wmo, wo, l, last)
        sa.append(nca.reshape(bs, 2, CONV_W))
        ss.append(ncs)
    y_sample = x.reshape(bs, 1, D_MODEL)

    return (y_prompt, y_sample, jnp.stack(pa), jnp.stack(ps), jnp.stack(ph),
            _cache_from_rows(pk_rows), _cache_from_rows(pv_rows),
            jnp.stack(sa), jnp.stack(ss).transpose(0, 2, 1, 3),
            s_ssm.reshape(depth, bs, SSM_HEADS, SSM_HEADDIM, SSM_STATE))
```

```python
import functools

import jax
import jax.numpy as jnp
from jax import lax
from jax.experimental import pallas as pl
from jax.experimental.pallas import tpu as pltpu

F32 = jnp.float32
BF16 = jnp.bfloat16

D_MODEL = 1024
CONV_W = 1024
SSM_INNER = 2048
SSM_HEADDIM = 64
SSM_HEADS = 32
SSM_GROUPS = 4
SSM_STATE = 128
SSM_XBC = SSM_INNER + 2 * SSM_GROUPS * SSM_STATE
HEADS_PER_GROUP = SSM_HEADS // SSM_GROUPS
GROUP_W = SSM_INNER // SSM_GROUPS
N_PAIRS = SSM_HEADS // 2
N_MEM = 256
MEM_HEADS = 4
MEM_HEADDIM = 256
ATTN_W = MEM_HEADS * MEM_HEADDIM
EPS = 1e-6
LOG2E = 1.4426950408889634
SSD_CHUNK = 128
LANES = 128
SUBLANES = 8
CACHE_ROWS = N_MEM * ATTN_W // LANES
VMEM_LIMIT = 58 * 1024 * 1024
WBLK = 1024
SIDE_W = 256
SSD_SEQS_PER_STEP = 8
ATTN_SEQS_PER_STEP = 8
STRIPS = WBLK // LANES

_B_CB, _B_CC, _B_CH, _B_CZ = 0, 1, 2, 3
_B_SZ = 4
_B_SX = 6
_B_DT = 9
_B_Q, _B_MZ = 11, 12
_B_GA, _B_GS, _B_GM = 13, 14, 15
_N_BLK = 16
_S_A = 0
_S_SZ = _B_SZ * WBLK
_S_SX = _B_SX * WBLK
_S_DT = _B_DT * WBLK
_S_M = _B_Q * WBLK
_S_G = _B_GA * WBLK
_S_END = _N_BLK * WBLK
_OFF_DT = 4 * CONV_W + SSM_INNER + SSM_XBC
_OFF_M = _OFF_DT + SSM_HEADS
_R_DA = SSM_INNER
_R_B = 2 * SSM_INNER
_R_C = _R_B + SSM_GROUPS * SSM_STATE
_R_END = _R_C + SSM_GROUPS * SSM_STATE

_V_NW = 0
_V_MNW = _V_NW + D_MODEL
_V_CSB = _V_MNW + D_MODEL
_V_DTB = _V_CSB + SSM_XBC
_V_ALOG = _V_DTB + LANES
_V_DSK = _V_ALOG + LANES
_V_SNW = _V_DSK + SSM_INNER
_V_DTBE = _V_SNW + SSM_INNER
_V_ALOGE = _V_DTBE + SSM_INNER
_V_FNW = _V_ALOGE + SSM_INNER
_V_END = _V_FNW + D_MODEL


def _dot(a, b):
    return jnp.dot(a, b, preferred_element_type=F32)


def _dot_nt(a, b):
    return lax.dot_general(a, b, (((1,), (1,)), ((), ())), preferred_element_type=F32)


def _rms(x, w):
    return x * lax.rsqrt(jnp.mean(x * x, axis=-1, keepdims=True) + EPS) * w


def _sigmoid(x):
    return 0.5 * jnp.tanh(0.5 * x) + 0.5


def _silu(x):
    h = 0.5 * x
    return h * jnp.tanh(h) + h


def _softplus(x):
    return jnp.maximum(x, 0.0) + jnp.log1p(jnp.exp(-jnp.abs(x)))


def _softmax_rows(s):
    m = jnp.max(s, axis=-1, keepdims=True)
    e = jnp.exp(s - m)
    return e / jnp.sum(e, axis=-1, keepdims=True)


def _split3(x):
    hi = x.astype(BF16)
    r1 = x - hi.astype(F32)
    mid = r1.astype(BF16)
    lo = (r1 - mid.astype(F32)).astype(BF16)
    return hi, mid, lo


def _vec(vec_ref, off, width):
    return vec_ref[:, off:off + width]


def _resident(block, index):
    return pl.BlockSpec(block, lambda *_: index, pipeline_mode=pl.Buffered(1))


_SRC_FIRST = (0,)


def _wsrc(blk):
    src = max(i for i, first in enumerate(_SRC_FIRST) if blk >= first)
    return src, blk - _SRC_FIRST[src]


def _wblk(layer, blk):
    return _resident((None, D_MODEL, WBLK), (layer, 0, _wsrc(blk)[1]))


def _wargs(wsrc, blocks):
    return [wsrc[_wsrc(b)[0]] for b in blocks]


def _vec_spec(layer):
    return _resident((None, 1, _V_END), (layer, 0, 0))


def _layer_mat(shape, layer):
    return _resident((None,) + shape, (layer, 0, 0))


def _params(sem):
    return pltpu.CompilerParams(dimension_semantics=sem, vmem_limit_bytes=VMEM_LIMIT)


def _kv_kernel(mem_ref, vec_ref, w_ref, k_ref, v_ref, kt_ref, vb_ref):
    u = _rms(mem_ref[...], _vec(vec_ref, _V_MNW, D_MODEL)).astype(BF16)
    kv = _dot(u, w_ref[...])
    k = kv[:, :ATTN_W]
    v = kv[:, ATTN_W:]
    halves = MEM_HEADDIM // LANES
    for j in range(halves):
        for h in range(MEM_HEADS):
            rows = pl.ds(j * MEM_HEADS + h, N_MEM, stride=halves * MEM_HEADS)
            sl = slice((h * halves + j) * LANES, (h * halves + j + 1) * LANES)
            k_ref[rows, :] = k[:, sl]
            v_ref[rows, :] = v[:, sl]
    kt_ref[...] = (k.T * (MEM_HEADDIM ** -0.5)).astype(BF16)
    vb_ref[...] = v.astype(BF16)


def _mem_kv(mem, vec, w_kv):
    depth = w_kv.shape[0]
    bt = mem.shape[0]
    out = lambda r, c: pl.BlockSpec((None, None, r, c), lambda l, b: (l, b, 0, 0))
    per_layer = lambda r, c: pl.BlockSpec((None, r, c), lambda l, b: (l, 0, 0))
    return pl.pallas_call(
        _kv_kernel,
        grid=(depth, bt),
        in_specs=[pl.BlockSpec((None, N_MEM, D_MODEL), lambda l, b: (b, 0, 0)), per_layer(1, _V_END),
                  per_layer(D_MODEL, 2 * ATTN_W)],
        out_specs=[out(CACHE_ROWS, LANES), out(CACHE_ROWS, LANES), out(ATTN_W, N_MEM), out(N_MEM, ATTN_W)],
        out_shape=[jax.ShapeDtypeStruct((depth, bt, CACHE_ROWS, LANES), F32),
                   jax.ShapeDtypeStruct((depth, bt, CACHE_ROWS, LANES), F32),
                   jax.ShapeDtypeStruct((depth, bt, ATTN_W, N_MEM), BF16),
                   jax.ShapeDtypeStruct((depth, bt, N_MEM, ATTN_W), BF16)],
        compiler_params=_params(("arbitrary", "arbitrary")),
        name="prompt_kv",
    )(mem, vec, w_kv)


def _cache_from_rows(c):
    depth, m = c.shape[:2]
    c = c.reshape(depth, m, N_MEM, MEM_HEADDIM // LANES, MEM_HEADS, LANES)
    return c.transpose(0, 1, 2, 4, 3, 5).reshape(depth, m, N_MEM, MEM_HEADS, MEM_HEADDIM)


def _am_kernel(x_ref, vec_ref, wcb_ref, wcc_ref, wch_ref, wcz_ref, wq_ref, wmz_ref, wga_ref, wgm_ref,
               caw_ref, wao_ref, wmo_ref, kt_ref, vb_ref,
               acc_ref, ca_ref, vbuf):
    c = pl.program_id(1)
    t = x_ref.shape[0]
    pad = SUBLANES

    @pl.when(c == 0)
    def _():
        vbuf[:, 0:pad, :] = jnp.zeros((STRIPS, pad, LANES), F32)

    ub = _rms(x_ref[...], _vec(vec_ref, _V_NW, D_MODEL)).astype(BF16)

    v = _dot(ub, wcc_ref[...]) * _dot(ub, wch_ref[...])
    gate = _silu(_dot(ub, wcz_ref[...])) * _dot(ub, wcb_ref[...])
    caw = caw_ref[...]
    acts = []
    for j in range(STRIPS):
        sl = slice(j * LANES, (j + 1) * LANES)
        vj = v[:, sl]
        vbuf[j, pad:pad + t, :] = vj
        conv = (caw[0:1, sl] * vbuf[j, pad - 2:pad - 2 + t, :] + caw[1:2, sl] * vbuf[j, pad - 1:pad - 1 + t, :]
                + caw[2:3, sl] * vj)
        acts.append((gate[:, sl] * conv).astype(BF16))
        ca_ref[:, sl] = vbuf[j, pad + t - 2:pad + t, :]
        vbuf[j, 0:pad, :] = vbuf[j, t:t + pad, :]
    y_a = _dot(jnp.concatenate(acts, axis=1), wao_ref[...])

    q = _dot(ub, wq_ref[...])
    mz = _dot(ub, wmz_ref[...])
    acts = []
    for h in range(MEM_HEADS):
        sl = slice(h * MEM_HEADDIM, (h + 1) * MEM_HEADDIM)
        p = _softmax_rows(_dot(q[:, sl].astype(BF16), kt_ref[sl, :]))
        o = _dot(p.astype(BF16), vb_ref[:, sl])
        acts.append((_silu(mz[:, sl]) * o).astype(BF16))
    y_m = _dot(jnp.concatenate(acts, axis=1), wmo_ref[...])

    acc_ref[...] = _sigmoid(_dot(ub, wga_ref[...])) * y_a + _sigmoid(_dot(ub, wgm_ref[...])) * y_m


def _prompt_am(x, vec, w_all, caw, wao, wmo, kt, vb, layer, tile):
    bt, seq, _ = x.shape
    nc = seq // tile
    tok = lambda w: pl.BlockSpec((None, tile, w), lambda b, c: (b, c, 0))
    per_b = lambda r, w: pl.BlockSpec((None, r, w), lambda b, c: (b, 0, 0))
    mem_kv = lambda r, w: pl.BlockSpec((None, None, r, w), lambda b, c: (layer, b, 0, 0))
    blocks = (_B_CB, _B_CC, _B_CH, _B_CZ, _B_Q, _B_MZ, _B_GA, _B_GM)
    return pl.pallas_call(
        _am_kernel,
        grid=(bt, nc),
        in_specs=([tok(D_MODEL), _vec_spec(layer)] + [_wblk(layer, b) for b in blocks]
                  + [_layer_mat((3, CONV_W), layer), _layer_mat((CONV_W, D_MODEL), layer),
                     _layer_mat((ATTN_W, D_MODEL), layer), mem_kv(ATTN_W, N_MEM), mem_kv(N_MEM, ATTN_W)]),
        out_specs=[tok(D_MODEL), per_b(2, CONV_W)],
        out_shape=[jax.ShapeDtypeStruct((bt, seq, D_MODEL), F32),
                   jax.ShapeDtypeStruct((bt, 2, CONV_W), F32)],
        scratch_shapes=[pltpu.VMEM((STRIPS, tile + SUBLANES, LANES), F32)],
        compiler_params=_params(("arbitrary", "arbitrary")),
        name="prompt_am",
    )(x, vec, *_wargs(w_all, blocks), caw, wao, wmo, kt, vb)


def _cumsum_rows(da):
    q = da.shape[0]
    r = lax.broadcasted_iota(jnp.int32, (q, q), 0)
    c = lax.broadcasted_iota(jnp.int32, (q, q), 1)
    tri = jnp.where(r >= c, 1.0, 0.0).astype(BF16)
    hi, mid, lo = _split3(da)
    return _dot(tri, hi) + _dot(tri, mid) + _dot(tri, lo)


def _ssd_kernel(x_ref, acc_ref, vec_ref, wsz0_ref, wsz1_ref, wsx0_ref, wsx1_ref, wsx2_ref, wgs_ref, wdt_ref,
                csw_ref, wso_ref, wo_ref,
                out_ref, cs_ref, ssm_ref,
                xbuf, tails, xbc_scr, dt_scr, ht_scr, *, final_norm):
    c = pl.program_id(1)
    t = x_ref.shape[0]
    q = SSD_CHUNK
    pad = SUBLANES

    @pl.when(c == 0)
    def _():
        tails[...] = jnp.zeros(tails.shape, F32)
        ht_scr[...] = jnp.zeros(ht_scr.shape, F32)

    ub = _rms(x_ref[...], _vec(vec_ref, _V_NW, D_MODEL)).astype(BF16)

    csw = csw_ref[...]
    for b, w_ref in enumerate((wsx0_ref, wsx1_ref, wsx2_ref)):
        sx = _dot(ub, w_ref[...])
        for j in range(STRIPS):
            s = b * STRIPS + j
            sl = slice(s * LANES, (s + 1) * LANES)
            sxj = sx[:, j * LANES:(j + 1) * LANES]
            xbuf[j, 0:pad, :] = tails[s]
            xbuf[j, pad:pad + t, :] = sxj
            yc = (csw[0:1, sl] * xbuf[j, pad - 3:pad - 3 + t, :] + csw[1:2, sl] * xbuf[j, pad - 2:pad - 2 + t, :]
                  + csw[2:3, sl] * xbuf[j, pad - 1:pad - 1 + t, :] + csw[3:4, sl] * sxj
                  + vec_ref[:, _V_CSB + s * LANES:_V_CSB + (s + 1) * LANES])
            xbc_scr[:, sl] = _silu(yc)
            cs_ref[:, sl] = xbuf[j, pad + t - 3:pad + t, :]
            tails[s] = xbuf[j, t:t + pad, :]
    dt_scr[...] = _softplus(_dot(ub, wdt_ref[...]) + _vec(vec_ref, _V_DTB, LANES))
    a_row = -jnp.exp(_vec(vec_ref, _V_ALOG, LANES))

    rr = lax.broadcasted_iota(jnp.int32, (q, q), 0)
    cc = lax.broadcasted_iota(jnp.int32, (q, q), 1)
    tril = rr >= cc
    left = lax.broadcasted_iota(jnp.int32, (1, LANES), 1) < SSM_HEADDIM

    def chunk(j, after_pair):
        rows = slice(j * q, (j + 1) * q)
        dtj = dt_scr[rows, :]
        cs = _cumsum_rows(dtj * a_row) * LOG2E
        cs_t = cs.T
        dt_t = dtj.T
        w_t = dt_t * jnp.exp2(cs_t[:, q - 1:q] - cs_t)
        etot = jnp.exp2(cs[q - 1:q, :])

        scores, bm_t, cms = [], [], []
        for g in range(SSM_GROUPS):
            bg = xbc_scr[rows, SSM_INNER + g * SSM_STATE:SSM_INNER + (g + 1) * SSM_STATE]
            cg = xbc_scr[rows, SSM_INNER + (SSM_GROUPS + g) * SSM_STATE:
                         SSM_INNER + (SSM_GROUPS + g + 1) * SSM_STATE]
            scores.append(_dot_nt(cg.astype(BF16), bg.astype(BF16)))
            bm_t.append(bg.T)
            cms.append(cg)

        for pair in range(N_PAIRS):
            g = (2 * pair) // HEADS_PER_GROUP
            lanes = slice(pair * LANES, (pair + 1) * LANES)
            xs_p = xbc_scr[rows, lanes]
            ht = ht_scr[pair]
            xsb = xs_p.astype(BF16)
            rhs = jnp.concatenate([xsb, ht.astype(BF16)], axis=0)
            ys, upds = [], []
            for h in (2 * pair, 2 * pair + 1):
                col = jnp.broadcast_to(cs[:, h:h + 1], (q, q))
                row = jnp.broadcast_to(cs_t[h:h + 1, :], (q, q))
                lmat = jnp.exp2(jnp.where(tril, col - row, -jnp.inf))
                lhs = jnp.concatenate([(scores[g] * lmat * dt_t[h:h + 1, :]).astype(BF16),
                                       (cms[g] * jnp.exp2(col)).astype(BF16)], axis=1)
                ys.append(_dot(lhs, rhs))
                upds.append(_dot((bm_t[g] * w_t[h:h + 1, :]).astype(BF16), xsb))
            xbc_scr[rows, lanes] = (jnp.where(left, ys[0], ys[1])
                                    + vec_ref[:, _V_DSK + pair * LANES:_V_DSK + (pair + 1) * LANES] * xs_p)
            dec = jnp.where(left, jnp.broadcast_to(etot[:, 2 * pair:2 * pair + 1], (1, LANES)),
                            jnp.broadcast_to(etot[:, 2 * pair + 1:2 * pair + 2], (1, LANES)))
            ht_scr[pair] = ht * dec + jnp.where(left, upds[0], upds[1])
            after_pair(j * N_PAIRS + pair)

    jobs = [(w_ref, k, act) for w_ref, act in ((wsz0_ref, _silu), (wsz1_ref, _silu), (wgs_ref, _sigmoid))
            for k in range(WBLK // SIDE_W)]
    n_slots = (t // q) * N_PAIRS
    job_at = {((2 * i + 1) * n_slots) // (2 * len(jobs)): i for i in range(len(jobs))}
    assert len(job_at) == len(jobs)
    side_vals = []

    def after_pair(slot):
        if slot in job_at:
            w_ref, k, act = jobs[job_at[slot]]
            side_vals.append(act(_dot(ub, w_ref[:, k * SIDE_W:(k + 1) * SIDE_W])))

    for j in range(t // q):
        chunk(j, after_pair)
    per_blk = WBLK // SIDE_W
    zgate = [jnp.concatenate(side_vals[b * per_blk:(b + 1) * per_blk], axis=1) for b in range(2)]
    g_s = jnp.concatenate(side_vals[2 * per_blk:], axis=1)

    @pl.when(c == pl.num_programs(1) - 1)
    def _():
        for pair in range(N_PAIRS):
            ssm_ref[pair * LANES:(pair + 1) * LANES, :] = ht_scr[pair].T

    parts = []
    for b in range(SSM_INNER // WBLK):
        gz = xbc_scr[:, b * WBLK:(b + 1) * WBLK] * zgate[b]
        for g in range(WBLK // GROUP_W):
            gg = gz[:, g * GROUP_W:(g + 1) * GROUP_W]
            off = _V_SNW + b * WBLK + g * GROUP_W
            parts.append((gg * lax.rsqrt(jnp.mean(gg * gg, axis=-1, keepdims=True) + EPS)
                          * vec_ref[:, off:off + GROUP_W]).astype(BF16))
    y_s = _dot(jnp.concatenate(parts, axis=1), wso_ref[...])
    merged = acc_ref[...] + g_s * y_s
    xn = x_ref[...] + _dot(merged.astype(BF16), wo_ref[...])
    out_ref[...] = _rms(xn, _vec(vec_ref, _V_FNW, D_MODEL)) if final_norm else xn


def _prompt_ssd(x, acc, vec, w_all, wdt, csw, wso, wo, layer, tile, final_norm):
    bt, seq, _ = x.shape
    nc = seq // tile
    tok = lambda w: pl.BlockSpec((None, tile, w), lambda b, c: (b, c, 0))
    per_b = lambda r, w: pl.BlockSpec((None, r, w), lambda b, c: (b, 0, 0))
    blocks = (_B_SZ, _B_SZ + 1, _B_SX, _B_SX + 1, _B_SX + 2, _B_GS)
    return pl.pallas_call(
        functools.partial(_ssd_kernel, final_norm=final_norm),
        grid=(bt, nc),
        in_specs=([tok(D_MODEL), tok(D_MODEL), _vec_spec(layer)] + [_wblk(layer, b) for b in blocks]
                  + [_layer_mat((D_MODEL, LANES), layer), _layer_mat((4, SSM_XBC), layer),
                     _layer_mat((SSM_INNER, D_MODEL), layer), _layer_mat((D_MODEL, D_MODEL), layer)]),
        out_specs=[tok(D_MODEL), per_b(3, SSM_XBC), per_b(SSM_HEADS * SSM_HEADDIM, SSM_STATE)],
        out_shape=[jax.ShapeDtypeStruct((bt, seq, D_MODEL), F32),
                   jax.ShapeDtypeStruct((bt, 3, SSM_XBC), F32),
                   jax.ShapeDtypeStruct((bt, SSM_HEADS * SSM_HEADDIM, SSM_STATE), F32)],
        scratch_shapes=[pltpu.VMEM((STRIPS, tile + SUBLANES, LANES), F32),
                        pltpu.VMEM((SSM_XBC // LANES, SUBLANES, LANES), F32),
                        pltpu.VMEM((tile, SSM_XBC), F32),
                        pltpu.VMEM((tile, LANES), F32),
                        pltpu.VMEM((N_PAIRS, SSM_STATE, LANES), F32)],
        compiler_params=_params(("arbitrary", "arbitrary")),
        name="prompt_ssd",
    )(x, acc, vec, *_wargs(w_all, blocks), wdt, csw, wso, wo)


def _proj_kernel(x_ref, vec_ref, *refs):
    w_refs, o_ref = refs[:-1], refs[-1]
    j = pl.program_id(0)
    ub = _rms(x_ref[...], _vec(vec_ref, _V_NW, D_MODEL)).astype(BF16)
    bounds = _SRC_FIRST + (_N_BLK,)
    for src, w_ref in enumerate(w_refs):
        @pl.when((j >= bounds[src]) & (j < bounds[src + 1]))
        def _(w_ref=w_ref):
            o_ref[...] = _dot(ub, w_ref[...])


def _sample_proj(x, vec, w_all, layer):
    m = x.shape[0]
    bounds = _SRC_FIRST + (_N_BLK,)

    def wspec(src):
        first, count = bounds[src], bounds[src + 1] - bounds[src]
        return pl.BlockSpec((None, D_MODEL, WBLK), lambda j: (layer, 0, jnp.clip(j - first, 0, count - 1)))

    return pl.pallas_call(
        _proj_kernel,
        grid=(_N_BLK,),
        in_specs=[_resident((m, D_MODEL), (0, 0)), _vec_spec(layer)] + [wspec(s) for s in range(len(w_all))],
        out_specs=pl.BlockSpec((m, WBLK), lambda j: (0, j)),
        out_shape=jax.ShapeDtypeStruct((m, _S_END), F32),
        compiler_params=_params(("arbitrary",)),
        name="sample_proj",
    )(x, vec, *w_all)


def _pre_kernel(p_ref, sca_ref, scs_ref, vec_ref, caw_ref, csw_ref,
                acta_ref, row_ref, xs_ref, nca_ref, ncs_ref):
    cb = p_ref[:, _S_A:_S_A + CONV_W]
    v = p_ref[:, _S_A + CONV_W:_S_A + 2 * CONV_W] * p_ref[:, _S_A + 2 * CONV_W:_S_A + 3 * CONV_W]
    cz = p_ref[:, _S_A + 3 * CONV_W:_S_A + 4 * CONV_W]
    caw = caw_ref[...]
    b0 = sca_ref[:, 0:CONV_W]
    b1 = sca_ref[:, CONV_W:2 * CONV_W]
    conv = caw[0:1] * b0 + caw[1:2] * b1 + caw[2:3] * v
    acta_ref[...] = _silu(cz) * cb * conv
    nca_ref[:, 0:CONV_W] = b1
    nca_ref[:, CONV_W:2 * CONV_W] = v

    sxbc = p_ref[:, _S_SX:_S_SX + SSM_XBC]
    csw = csw_ref[...]
    s0 = scs_ref[0]
    s1 = scs_ref[1]
    s2 = scs_ref[2]
    xbc = _silu(csw[0:1] * s0 + csw[1:2] * s1 + csw[2:3] * s2 + csw[3:4] * sxbc + _vec(vec_ref, _V_CSB, SSM_XBC))
    ncs_ref[0] = s1
    ncs_ref[1] = s2
    ncs_ref[2] = sxbc
    xs = xbc[:, :SSM_INNER]
    dt = _softplus(p_ref[:, _S_DT:_S_DT + SSM_INNER] + _vec(vec_ref, _V_DTBE, SSM_INNER))
    xs_ref[...] = xs
    row_ref[:, 0:_R_DA] = xs * dt
    row_ref[:, _R_DA:_R_B] = jnp.exp(dt * -jnp.exp(_vec(vec_ref, _V_ALOGE, SSM_INNER)))
    row_ref[:, _R_B:_R_END] = xbc[:, SSM_INNER:]


def _sample_pre(proj, sca, scs_all, vec, caw, csw, layer, rb):
    m = proj.shape[0]
    taps = scs_all.shape[1]
    rows = lambda w: pl.BlockSpec((rb, w), lambda i: (i, 0))
    return pl.pallas_call(
        _pre_kernel,
        grid=(m // rb,),
        in_specs=[rows(_S_END), rows(2 * CONV_W),
                  pl.BlockSpec((None, taps, rb, SSM_XBC), lambda i: (layer, 0, i, 0)), _vec_spec(layer),
                  _layer_mat((3, CONV_W), layer), _layer_mat((4, SSM_XBC), layer)],
        out_specs=[rows(CONV_W), rows(_R_END), rows(SSM_INNER), rows(2 * CONV_W),
                   pl.BlockSpec((taps, rb, SSM_XBC), lambda i: (0, i, 0))],
        out_shape=[jax.ShapeDtypeStruct((m, CONV_W), F32),
                   jax.ShapeDtypeStruct((m, _R_END), F32),
                   jax.ShapeDtypeStruct((m, SSM_INNER), F32),
                   jax.ShapeDtypeStruct((m, 2 * CONV_W), F32),
                   jax.ShapeDtypeStruct((taps, m, SSM_XBC), F32)],
        compiler_params=_params(("arbitrary",)),
        name="sample_pre",
    )(proj, sca, scs_all, vec, caw, csw)


def _sssd_kernel(row_ref, h0_ref, *rest):
    hn_ref, y_ref = rest[-2:]
    nb = row_ref.shape[0]
    sub_x = lax.broadcasted_iota(jnp.int32, (LANES, GROUP_W), 0)
    sub_r = lax.broadcasted_iota(jnp.int32, (LANES, 2 * SSM_STATE), 0)
    eye = (lax.broadcasted_iota(jnp.int32, (LANES, SSM_STATE), 0)
           == lax.broadcasted_iota(jnp.int32, (LANES, SSM_STATE), 1))
    zero = jnp.zeros((1, SSM_STATE), F32)
    for i in range(nb):
        row = row_ref[i]
        xparts = []
        rk = jnp.zeros((LANES, 2 * SSM_STATE), F32)
        for g in range(SSM_GROUPS):
            xdt = row[:, g * GROUP_W:(g + 1) * GROUP_W]
            x_hi = xdt.astype(BF16).astype(F32)
            x_lo = xdt - x_hi
            da = row[:, _R_DA + g * GROUP_W:_R_DA + (g + 1) * GROUP_W]
            a_hi = da.astype(BF16).astype(F32)
            a_mid = (da - a_hi).astype(BF16).astype(F32)
            a_lo = da - a_hi - a_mid
            k = sub_x - g * SUBLANES
            xparts.append(jnp.where((k == 0) | (k == 2), x_hi,
                                    jnp.where((k == 1) | (k == 3), x_lo,
                                              jnp.where(k == 4, a_hi,
                                                        jnp.where(k == 5, a_mid,
                                                                  jnp.where(k == 6, a_lo, 0.0))))))
            bg = row[:, _R_B + g * SSM_STATE:_R_B + (g + 1) * SSM_STATE]
            b_hi = bg.astype(BF16).astype(F32)
            b_lo = bg - b_hi
            kr = sub_r - g * SUBLANES
            rk = jnp.where((kr == 0) | (kr == 1), jnp.concatenate([b_hi, zero], axis=1),
                           jnp.where((kr == 2) | (kr == 3), jnp.concatenate([b_lo, zero], axis=1),
                                     jnp.where((kr >= 4) & (kr < 7), jnp.concatenate([zero, zero + 1.0], axis=1),
                                               rk)))
        xk = jnp.concatenate(xparts, axis=1)
        bc = _dot(xk.T.astype(BF16), rk.astype(BF16))
        hnew = bc[:, SSM_STATE:] * h0_ref[i] + bc[:, :SSM_STATE]
        hn_ref[i] = hnew
        hb = hnew.astype(BF16)
        ys = []
        for g in range(SSM_GROUPS):
            cg = row[:, _R_C + g * SSM_STATE:_R_C + (g + 1) * SSM_STATE]
            c_cols = jnp.broadcast_to(cg, (SSM_STATE, SSM_STATE)).T.astype(BF16)
            rep = _dot(hb[g * GROUP_W:(g + 1) * GROUP_W, :], c_cols)
            for k in range(GROUP_W // LANES):
                ys.append(jnp.sum(jnp.where(eye, rep[k * LANES:(k + 1) * LANES, :], 0.0), axis=0, keepdims=True))
        y_ref[i] = jnp.concatenate(ys, axis=1)


def _sample_ssd(rows, state_all, layer, stacked, nb):
    depth, m, hp, _ = state_all.shape
    state_spec = pl.BlockSpec((None, nb, hp, SSM_STATE), lambda i: (layer, i, 0, 0))
    in_specs = [pl.BlockSpec((nb, 1, _R_END), lambda i: (i, 0, 0)), state_spec]
    args = [rows.reshape(m, 1, _R_END), state_all]
    aliases = {}
    if stacked is not None:
        in_specs.append(pl.BlockSpec(memory_space=pl.ANY))
        args.append(stacked)
        aliases = {2: 0}
    return pl.pallas_call(
        _sssd_kernel,
        grid=(m // nb,),
        in_specs=in_specs,
        out_specs=[state_spec, pl.BlockSpec((nb, 1, hp), lambda i: (i, 0, 0))],
        out_shape=[jax.ShapeDtypeStruct((depth, m, hp, SSM_STATE), F32),
                   jax.ShapeDtypeStruct((m, 1, hp), F32)],
        input_output_aliases=aliases,
        compiler_params=_params(("arbitrary",)),
        name="sample_ssd",
    )(*args)


def _sattn_kernel(q_ref, k_ref, v_ref, o_ref):
    nb = q_ref.shape[0]
    halves = MEM_HEADDIM // LANES
    rows = halves * MEM_HEADS
    row = lax.broadcasted_iota(jnp.int32, (rows, CACHE_ROWS), 0)
    col = lax.broadcasted_iota(jnp.int32, (rows, CACHE_ROWS), 1)
    own = (col & (rows - 1)) == row
    lane = lax.broadcasted_iota(jnp.int32, (1, LANES), 1)
    low_half = (lane & MEM_HEADS) == 0
    for i in range(nb):
        q = q_ref[i]
        q8 = jnp.concatenate([q[:, (h * halves + j) * LANES:(h * halves + j + 1) * LANES]
                              for j in range(halves) for h in range(MEM_HEADS)], axis=0)
        s = _dot_nt(q8.astype(BF16), k_ref[i].astype(BF16))
        t = jnp.sum(jnp.where(own, s, 0.0), axis=0, keepdims=True)
        parts = []
        for b in range(CACHE_ROWS // LANES):
            tb = t[:, b * LANES:(b + 1) * LANES]
            parts.append(tb + jnp.where(low_half, pltpu.roll(tb, LANES - MEM_HEADS, 1),
                                        pltpu.roll(tb, MEM_HEADS, 1)))
        u = jnp.concatenate(parts, axis=1) * (MEM_HEADDIM ** -0.5)
        p = _softmax_rows(jnp.where(own, u, -jnp.inf))
        o8 = _dot(p.astype(BF16), v_ref[i].astype(BF16))
        for j in range(halves):
            for h in range(MEM_HEADS):
                r = j * MEM_HEADS + h
                o_ref[i, :, (h * halves + j) * LANES:(h * halves + j + 1) * LANES] = o8[r:r + 1, :]


def _cache_rows(c):
    depth, m = c.shape[:2]
    c = c.reshape(depth, m, N_MEM, MEM_HEADS, MEM_HEADDIM // LANES, LANES)
    return c.transpose(0, 1, 2, 4, 3, 5).reshape(depth, m, CACHE_ROWS, LANES)


def _sample_attn(q, k_rows, v_rows, layer, nb):
    m = q.shape[0]
    cache = pl.BlockSpec((None, nb, CACHE_ROWS, LANES), lambda i: (layer, i, 0, 0))
    return pl.pallas_call(
        _sattn_kernel,
        grid=(m // nb,),
        in_specs=[pl.BlockSpec((nb, 1, ATTN_W), lambda i: (i, 0, 0)), cache, cache],
        out_specs=pl.BlockSpec((nb, 1, ATTN_W), lambda i: (i, 0, 0)),
        out_shape=jax.ShapeDtypeStruct((m, 1, ATTN_W), F32),
        compiler_params=_params(("arbitrary",)),
        name="sample_attn",
    )(q.reshape(m, 1, ATTN_W), k_rows, v_rows)


def _post_kernel(x_ref, p_ref, acta_ref, xs_ref, ysd_ref, o_ref, vec_ref, wao_ref, wso_ref, wmo_ref, wo_ref,
                 out_ref, *, final_norm):
    y_a = _dot(acta_ref[...].astype(BF16), wao_ref[...])
    gz = ((ysd_ref[...] + _vec(vec_ref, _V_DSK, SSM_INNER) * xs_ref[...])
          * _silu(p_ref[:, _S_SZ:_S_SZ + SSM_INNER]))
    parts = []
    for g in range(SSM_GROUPS):
        gg = gz[:, g * GROUP_W:(g + 1) * GROUP_W]
        parts.append(gg * lax.rsqrt(jnp.mean(gg * gg, axis=-1, keepdims=True) + EPS))
    gn = jnp.concatenate(parts, axis=1) * _vec(vec_ref, _V_SNW, SSM_INNER)
    y_s = _dot(gn.astype(BF16), wso_ref[...])
    mz = p_ref[:, _S_M + ATTN_W:_S_M + 2 * ATTN_W]
    y_m = _dot((_silu(mz) * o_ref[...]).astype(BF16), wmo_ref[...])
    merged = (_sigmoid(p_ref[:, _S_G:_S_G + D_MODEL]) * y_a
              + _sigmoid(p_ref[:, _S_G + D_MODEL:_S_G + 2 * D_MODEL]) * y_s
              + _sigmoid(p_ref[:, _S_G + 2 * D_MODEL:_S_G + 3 * D_MODEL]) * y_m)
    xn = x_ref[...] + _dot(merged.astype(BF16), wo_ref[...])
    out_ref[...] = _rms(xn, _vec(vec_ref, _V_FNW, D_MODEL)) if final_norm else xn


def _sample_post(x, proj, acta, xs, ysd, o, vec, wao, wso, wmo, wo, layer, final_norm):
    m = x.shape[0]
    full = lambda a: _resident(a.shape, (0,) * a.ndim)
    acts = (x, proj, acta, xs, ysd, o)
    return pl.pallas_call(
        functools.partial(_post_kernel, final_norm=final_norm),
        grid=(1,),
        in_specs=([full(a) for a in acts]
                  + [_vec_spec(layer), _layer_mat((CONV_W, D_MODEL), layer), _layer_mat((SSM_INNER, D_MODEL), layer),
                     _layer_mat((ATTN_W, D_MODEL), layer), _layer_mat((D_MODEL, D_MODEL), layer)]),
        out_specs=_resident((m, D_MODEL), (0, 0)),
        out_shape=jax.ShapeDtypeStruct((m, D_MODEL), F32),
        compiler_params=_params(("arbitrary",)),
        name="sample_post",
    )(*acts, vec, wao, wso, wmo, wo)


_TAIL_FIRST_A = _OFF_DT // WBLK
_TAIL_FIRST_B = _OFF_DT // LANES


def _regroup_kernel(a_ref, b_ref, last_ref, o_ref):
    j = pl.program_id(1)

    @pl.when(j < _B_DT)
    def _():
        o_ref[...] = a_ref[...].T.astype(BF16)

    @pl.when((j >= _B_DT) & (j < _B_Q))
    def _():
        heads = WBLK // SSM_HEADDIM
        first = (j - _B_DT) * heads
        rows = jnp.concatenate([jnp.broadcast_to(b_ref[pl.ds(first + k, 1), :], (SSM_HEADDIM, D_MODEL))
                                for k in range(heads)], axis=0)
        o_ref[...] = rows.T.astype(BF16)

    @pl.when(j >= _B_Q)
    def _():
        nxt = jnp.where(j == _N_BLK - 1, last_ref[...], b_ref[...])
        rows = jnp.concatenate([a_ref[SSM_HEADS:, :], nxt[:SSM_HEADS, :]], axis=0)
        o_ref[...] = rows.T.astype(BF16)


def _regroup_w_in(w_in):
    depth, _, n_in = w_in.shape
    w_in = w_in.transpose(0, 2, 1)
    last = jnp.pad(w_in[:, n_in - SSM_HEADS:, :], ((0, 0), (0, LANES - SSM_HEADS), (0, 0)))
    a_idx = lambda j: jnp.where(j < _B_DT, j, jnp.clip(j - _B_Q + _TAIL_FIRST_A, _TAIL_FIRST_A, n_in // WBLK - 1))
    b_idx = lambda j: jnp.where(j < _B_Q, _TAIL_FIRST_B,
                                jnp.minimum(_TAIL_FIRST_B + (WBLK // LANES) * (j - _B_Q + 1), n_in // LANES - 1))
    return pl.pallas_call(
        _regroup_kernel,
        grid=(depth, _N_BLK),
        in_specs=[pl.BlockSpec((None, WBLK, D_MODEL), lambda l, j: (l, a_idx(j), 0)),
                  pl.BlockSpec((None, LANES, D_MODEL), lambda l, j: (l, b_idx(j), 0)),
                  pl.BlockSpec((None, LANES, D_MODEL), lambda l, j: (l, 0, 0))],
        out_specs=pl.BlockSpec((None, D_MODEL, WBLK), lambda l, j: (l, 0, j)),
        out_shape=jax.ShapeDtypeStruct((depth, D_MODEL, _S_END), BF16),
        compiler_params=_params(("arbitrary", "arbitrary")),
        name="regroup_w_in",
    )(w_in, w_in, last)


def _pad_lanes(a):
    return jnp.pad(a, ((0, 0), (0, LANES - a.shape[1])))


def _prompt_tile(seq):
    for t in (512, 256, 128):
        if seq % t == 0:
            return t
    raise ValueError(f"sequence length {seq} must be a multiple of {SSD_CHUNK}")


def kernel(x_prompt, x_sample, mem_prompt, state_conv_a, state_conv_s, state_ssm, cache_mem_k, cache_mem_v,
           norm_w, w_in, conv_a_w, w_a_out, conv_s_w, conv_s_b, dt_bias, a_log, d_skip, ssm_norm_w,
           w_s_out, mem_norm_w, w_mem_kv, w_m_out, w_o, final_norm_w):
    depth = w_in.shape[0]
    bp, seq, _ = x_prompt.shape
    bs = x_sample.shape[0]
    assert x_sample.shape[1] == 1
    tile = _prompt_tile(seq)

    w_all = (_regroup_w_in(w_in),)
    wdt = jnp.pad(w_in[:, :, _OFF_DT:_OFF_M].astype(BF16), ((0, 0), (0, 0), (0, LANES - SSM_HEADS)))
    wao, wso, wmo, wo, wkv = (a.astype(BF16) for a in (w_a_out, w_s_out, w_m_out, w_o, w_mem_kv))
    rep = lambda a: jnp.repeat(a, SSM_HEADDIM, axis=1)
    vec = jnp.concatenate([norm_w, mem_norm_w, conv_s_b, _pad_lanes(dt_bias), _pad_lanes(a_log), rep(d_skip),
                           ssm_norm_w, rep(dt_bias), rep(a_log),
                           jnp.broadcast_to(final_norm_w[None], (depth, D_MODEL))], axis=1)[:, None, :]

    x = x_prompt
    pa, ps, ph = [], [], []
    pk_rows, pv_rows, kt, vb = _mem_kv(mem_prompt, vec, wkv)
    for l in range(depth):
        last = l == depth - 1
        acc, ca = _prompt_am(x, vec, w_all, conv_a_w, wao, wmo, kt, vb, l, tile)
        x, cs, hs = _prompt_ssd(x, acc, vec, w_all, wdt, conv_s_w, wso, wo, l, tile, last)
        pa.append(ca)
        ps.append(cs)
        ph.append(hs.reshape(bp, SSM_HEADS, SSM_HEADDIM, SSM_STATE))
    y_prompt = x

    x = x_sample.reshape(bs, D_MODEL)
    state_all = state_ssm.reshape(depth, bs, SSM_HEADS * SSM_HEADDIM, SSM_STATE)
    k_rows, v_rows = _cache_rows(cache_mem_k), _cache_rows(cache_mem_v)
    scs_all = state_conv_s.transpose(0, 2, 1, 3)
    sa, ss, s_ssm = [], [], None
    for l in range(depth):
        last = l == depth - 1
        proj = _sample_proj(x, vec, w_all, l)
        acta, rows, xs, nca, ncs = _sample_pre(
            proj, state_conv_a[l].reshape(bs, 2 * CONV_W), scs_all, vec, conv_a_w, conv_s_w, l, 32)
        s_ssm, ysd = _sample_ssd(rows, state_all, l, s_ssm, SSD_SEQS_PER_STEP)
        o = _sample_attn(proj[:, _S_M:_S_M + ATTN_W], k_rows, v_rows, l, ATTN_SEQS_PER_STEP)
        x = _sample_post(x, proj, acta, xs, ysd.reshape(bs, SSM_INNER), o.reshape(bs, ATTN_W), vec,
                         wao, wso, wmo, wo, l, last)
        sa.append(nca.reshape(bs, 2, CONV_W))
        ss.append(ncs)
    y_sample = x.reshape(bs, 1, D_MODEL)

    return (y_prompt, y_sample, jnp.stack(pa), jnp.stack(ps), jnp.stack(ph),
            _cache_from_rows(pk_rows), _cache_from_rows(pv_rows),
            jnp.stack(sa), jnp.stack(ss).transpose(0, 2, 1, 3),
            s_ssm.reshape(depth, bs, SSM_HEADS, SSM_HEADDIM, SSM_STATE))
```

```python
import functools

import jax
import jax.numpy as jnp
from jax import lax
from jax.experimental import pallas as pl
from jax.experimental.pallas import tpu as pltpu

F32 = jnp.float32
BF16 = jnp.bfloat16

D_MODEL = 1024
CONV_W = 1024
SSM_INNER = 2048
SSM_HEADDIM = 64
SSM_HEADS = 32
SSM_GROUPS = 4
SSM_STATE = 128
SSM_XBC = SSM_INNER + 2 * SSM_GROUPS * SSM_STATE
HEADS_PER_GROUP = SSM_HEADS // SSM_GROUPS
GROUP_W = SSM_INNER // SSM_GROUPS
N_PAIRS = SSM_HEADS // 2
N_MEM = 256
MEM_HEADS = 4
MEM_HEADDIM = 256
ATTN_W = MEM_HEADS * MEM_HEADDIM
EPS = 1e-6
LOG2E = 1.4426950408889634
SSD_CHUNK = 128
LANES = 128
SUBLANES = 8
CACHE_ROWS = N_MEM * ATTN_W // LANES
VMEM_LIMIT = 58 * 1024 * 1024
WBLK = 1024
SIDE_W = 256
SSD_SEQS_PER_STEP = 8
ATTN_SEQS_PER_STEP = 8
STRIPS = WBLK // LANES

_B_CB, _B_CC, _B_CH, _B_CZ = 0, 1, 2, 3
_B_SZ = 4
_B_SX = 6
_B_DT = 9
_B_Q, _B_MZ = 11, 12
_B_GA, _B_GS, _B_GM = 13, 14, 15
_N_BLK = 16
_S_A = 0
_S_SZ = _B_SZ * WBLK
_S_SX = _B_SX * WBLK
_S_DT = _B_DT * WBLK
_S_M = _B_Q * WBLK
_S_G = _B_GA * WBLK
_S_END = _N_BLK * WBLK
_OFF_DT = 4 * CONV_W + SSM_INNER + SSM_XBC
_OFF_M = _OFF_DT + SSM_HEADS
_R_DA = SSM_INNER
_R_B = 2 * SSM_INNER
_R_C = _R_B + SSM_GROUPS * SSM_STATE
_R_END = _R_C + SSM_GROUPS * SSM_STATE

_V_NW = 0
_V_MNW = _V_NW + D_MODEL
_V_CSB = _V_MNW + D_MODEL
_V_DTB = _V_CSB + SSM_XBC
_V_ALOG = _V_DTB + LANES
_V_DSK = _V_ALOG + LANES
_V_SNW = _V_DSK + SSM_INNER
_V_DTBE = _V_SNW + SSM_INNER
_V_ALOGE = _V_DTBE + SSM_INNER
_V_FNW = _V_ALOGE + SSM_INNER
_V_END = _V_FNW + D_MODEL


def _dot(a, b):
    return jnp.dot(a, b, preferred_element_type=F32)


def _dot_nt(a, b):
    return lax.dot_general(a, b, (((1,), (1,)), ((), ())), preferred_element_type=F32)


def _rms(x, w):
    return x * lax.rsqrt(jnp.mean(x * x, axis=-1, keepdims=True) + EPS) * w


def _sigmoid(x):
    return 0.5 * jnp.tanh(0.5 * x) + 0.5


def _silu(x):
    h = 0.5 * x
    return h * jnp.tanh(h) + h


def _softplus(x):
    return jnp.maximum(x, 0.0) + jnp.log1p(jnp.exp(-jnp.abs(x)))


def _softmax_rows(s):
    m = jnp.max(s, axis=-1, keepdims=True)
    e = jnp.exp(s - m)
    return e / jnp.sum(e, axis=-1, keepdims=True)


def _split3(x):
    hi = x.astype(BF16)
    r1 = x - hi.astype(F32)
    mid = r1.astype(BF16)
    lo = (r1 - mid.astype(F32)).astype(BF16)
    return hi, mid, lo


def _vec(vec_ref, off, width):
    return vec_ref[:, off:off + width]


def _resident(block, index):
    return pl.BlockSpec(block, lambda *_: index, pipeline_mode=pl.Buffered(1))


_SRC_FIRST = (0,)


def _wsrc(blk):
    src = max(i for i, first in enumerate(_SRC_FIRST) if blk >= first)
    return src, blk - _SRC_FIRST[src]


def _wblk(layer, blk):
    return _resident((None, D_MODEL, WBLK), (layer, 0, _wsrc(blk)[1]))


def _wargs(wsrc, blocks):
    return [wsrc[_wsrc(b)[0]] for b in blocks]


def _vec_spec(layer):
    return _resident((None, 1, _V_END), (layer, 0, 0))


def _layer_mat(shape, layer):
    return _resident((None,) + shape, (layer, 0, 0))


def _params(sem):
    return pltpu.CompilerParams(dimension_semantics=sem, vmem_limit_bytes=VMEM_LIMIT)


def _kv_kernel(mem_ref, vec_ref, w_ref, k_ref, v_ref, kt_ref, vb_ref):
    u = _rms(mem_ref[...], _vec(vec_ref, _V_MNW, D_MODEL)).astype(BF16)
    kv = _dot(u, w_ref[...])
    k = kv[:, :ATTN_W]
    v = kv[:, ATTN_W:]
    halves = MEM_HEADDIM // LANES
    for j in range(halves):
        for h in range(MEM_HEADS):
            rows = pl.ds(j * MEM_HEADS + h, N_MEM, stride=halves * MEM_HEADS)
            sl = slice((h * halves + j) * LANES, (h * halves + j + 1) * LANES)
            k_ref[rows, :] = k[:, sl]
            v_ref[rows, :] = v[:, sl]
    kt_ref[...] = (k.T * (MEM_HEADDIM ** -0.5)).astype(BF16)
    vb_ref[...] = v.astype(BF16)


def _mem_kv(mem, vec, w_kv):
    depth = w_kv.shape[0]
    bt = mem.shape[0]
    out = lambda r, c: pl.BlockSpec((None, None, r, c), lambda l, b: (l, b, 0, 0))
    per_layer = lambda r, c: pl.BlockSpec((None, r, c), lambda l, b: (l, 0, 0))
    return pl.pallas_call(
        _kv_kernel,
        grid=(depth, bt),
        in_specs=[pl.BlockSpec((None, N_MEM, D_MODEL), lambda l, b: (b, 0, 0)), per_layer(1, _V_END),
                  per_layer(D_MODEL, 2 * ATTN_W)],
        out_specs=[out(CACHE_ROWS, LANES), out(CACHE_ROWS, LANES), out(ATTN_W, N_MEM), out(N_MEM, ATTN_W)],
        out_shape=[jax.ShapeDtypeStruct((depth, bt, CACHE_ROWS, LANES), F32),
                   jax.ShapeDtypeStruct((depth, bt, CACHE_ROWS, LANES), F32),
                   jax.ShapeDtypeStruct((depth, bt, ATTN_W, N_MEM), BF16),
                   jax.ShapeDtypeStruct((depth, bt, N_MEM, ATTN_W), BF16)],
        compiler_params=_params(("arbitrary", "arbitrary")),
        name="prompt_kv",
    )(mem, vec, w_kv)


def _cache_from_rows(c):
    depth, m = c.shape[:2]
    c = c.reshape(depth, m, N_MEM, MEM_HEADDIM // LANES, MEM_HEADS, LANES)
    return c.transpose(0, 1, 2, 4, 3, 5).reshape(depth, m, N_MEM, MEM_HEADS, MEM_HEADDIM)


def _am_kernel(x_ref, vec_ref, wcb_ref, wcc_ref, wch_ref, wcz_ref, wq_ref, wmz_ref, wga_ref, wgm_ref,
               caw_ref, wao_ref, wmo_ref, kt_ref, vb_ref,
               acc_ref, ca_ref, vbuf):
    c = pl.program_id(1)
    t = x_ref.shape[0]
    pad = SUBLANES

    @pl.when(c == 0)
    def _():
        vbuf[:, 0:pad, :] = jnp.zeros((STRIPS, pad, LANES), F32)

    ub = _rms(x_ref[...], _vec(vec_ref, _V_NW, D_MODEL)).astype(BF16)

    v = _dot(ub, wcc_ref[...]) * _dot(ub, wch_ref[...])
    gate = _silu(_dot(ub, wcz_ref[...])) * _dot(ub, wcb_ref[...])
    caw = caw_ref[...]
    acts = []
    for j in range(STRIPS):
        sl = slice(j * LANES, (j + 1) * LANES)
        vj = v[:, sl]
        vbuf[j, pad:pad + t, :] = vj
        conv = (caw[0:1, sl] * vbuf[j, pad - 2:pad - 2 + t, :] + caw[1:2, sl] * vbuf[j, pad - 1:pad - 1 + t, :]
                + caw[2:3, sl] * vj)
        acts.append((gate[:, sl] * conv).astype(BF16))
        ca_ref[:, sl] = vbuf[j, pad + t - 2:pad + t, :]
        vbuf[j, 0:pad, :] = vbuf[j, t:t + pad, :]
    y_a = _dot(jnp.concatenate(acts, axis=1), wao_ref[...])

    q = _dot(ub, wq_ref[...])
    mz = _dot(ub, wmz_ref[...])
    acts = []
    for h in range(MEM_HEADS):
        sl = slice(h * MEM_HEADDIM, (h + 1) * MEM_HEADDIM)
        p = _softmax_rows(_dot(q[:, sl].astype(BF16), kt_ref[sl, :]))
        o = _dot(p.astype(BF16), vb_ref[:, sl])
        acts.append((_silu(mz[:, sl]) * o).astype(BF16))
    y_m = _dot(jnp.concatenate(acts, axis=1), wmo_ref[...])

    acc_ref[...] = _sigmoid(_dot(ub, wga_ref[...])) * y_a + _sigmoid(_dot(ub, wgm_ref[...])) * y_m


def _prompt_am(x, vec, w_all, caw, wao, wmo, kt, vb, layer, tile):
    bt, seq, _ = x.shape
    nc = seq // tile
    tok = lambda w: pl.BlockSpec((None, tile, w), lambda b, c: (b, c, 0))
    per_b = lambda r, w: pl.BlockSpec((None, r, w), lambda b, c: (b, 0, 0))
    mem_kv = lambda r, w: pl.BlockSpec((None, None, r, w), lambda b, c: (layer, b, 0, 0))
    blocks = (_B_CB, _B_CC, _B_CH, _B_CZ, _B_Q, _B_MZ, _B_GA, _B_GM)
    return pl.pallas_call(
        _am_kernel,
        grid=(bt, nc),
        in_specs=([tok(D_MODEL), _vec_spec(layer)] + [_wblk(layer, b) for b in blocks]
                  + [_layer_mat((3, CONV_W), layer), _layer_mat((CONV_W, D_MODEL), layer),
                     _layer_mat((ATTN_W, D_MODEL), layer), mem_kv(ATTN_W, N_MEM), mem_kv(N_MEM, ATTN_W)]),
        out_specs=[tok(D_MODEL), per_b(2, CONV_W)],
        out_shape=[jax.ShapeDtypeStruct((bt, seq, D_MODEL), F32),
                   jax.ShapeDtypeStruct((bt, 2, CONV_W), F32)],
        scratch_shapes=[pltpu.VMEM((STRIPS, tile + SUBLANES, LANES), F32)],
        compiler_params=_params(("arbitrary", "arbitrary")),
        name="prompt_am",
    )(x, vec, *_wargs(w_all, blocks), caw, wao, wmo, kt, vb)


def _cumsum_rows(da):
    q = da.shape[0]
    r = lax.broadcasted_iota(jnp.int32, (q, q), 0)
    c = lax.broadcasted_iota(jnp.int32, (q, q), 1)
    tri = jnp.where(r >= c, 1.0, 0.0).astype(BF16)
    hi, mid, lo = _split3(da)
    return _dot(tri, hi) + _dot(tri, mid) + _dot(tri, lo)


def _ssd_kernel(x_ref, acc_ref, vec_ref, wsz0_ref, wsz1_ref, wsx0_ref, wsx1_ref, wsx2_ref, wgs_ref, wdt_ref,
                csw_ref, wso_ref, wo_ref,
                out_ref, cs_ref, ssm_ref,
                xbuf, tails, xbc_scr, dt_scr, ht_scr, *, final_norm):
    c = pl.program_id(1)
    t = x_ref.shape[0]
    q = SSD_CHUNK
    pad = SUBLANES

    @pl.when(c == 0)
    def _():
        tails[...] = jnp.zeros(tails.shape, F32)
        ht_scr[...] = jnp.zeros(ht_scr.shape, F32)

    ub = _rms(x_ref[...], _vec(vec_ref, _V_NW, D_MODEL)).astype(BF16)

    csw = csw_ref[...]
    for b, w_ref in enumerate((wsx0_ref, wsx1_ref, wsx2_ref)):
        sx = _dot(ub, w_ref[...])
        for j in range(STRIPS):
            s = b * STRIPS + j
            sl = slice(s * LANES, (s + 1) * LANES)
            sxj = sx[:, j * LANES:(j + 1) * LANES]
            xbuf[j, 0:pad, :] = tails[s]
            xbuf[j, pad:pad + t, :] = sxj
            yc = (csw[0:1, sl] * xbuf[j, pad - 3:pad - 3 + t, :] + csw[1:2, sl] * xbuf[j, pad - 2:pad - 2 + t, :]
                  + csw[2:3, sl] * xbuf[j, pad - 1:pad - 1 + t, :] + csw[3:4, sl] * sxj
                  + vec_ref[:, _V_CSB + s * LANES:_V_CSB + (s + 1) * LANES])
            xbc_scr[:, sl] = _silu(yc)
            cs_ref[:, sl] = xbuf[j, pad + t - 3:pad + t, :]
            tails[s] = xbuf[j, t:t + pad, :]
    dt_scr[...] = _softplus(_dot(ub, wdt_ref[...]) + _vec(vec_ref, _V_DTB, LANES))
    a_row = -jnp.exp(_vec(vec_ref, _V_ALOG, LANES))

    rr = lax.broadcasted_iota(jnp.int32, (q, q), 0)
    cc = lax.broadcasted_iota(jnp.int32, (q, q), 1)
    tril = rr >= cc
    left = lax.broadcasted_iota(jnp.int32, (1, LANES), 1) < SSM_HEADDIM

    def chunk_prep(j):
        rows = slice(j * q, (j + 1) * q)
        dtj = dt_scr[rows, :]
        cs = _cumsum_rows(dtj * a_row) * LOG2E
        cs_t = cs.T
        dt_t = dtj.T
        w_t = dt_t * jnp.exp2(cs_t[:, q - 1:q] - cs_t)
        etot = jnp.exp2(cs[q - 1:q, :])

        scores, bm_t, cms = [], [], []
        for g in range(SSM_GROUPS):
            bg = xbc_scr[rows, SSM_INNER + g * SSM_STATE:SSM_INNER + (g + 1) * SSM_STATE]
            cg = xbc_scr[rows, SSM_INNER + (SSM_GROUPS + g) * SSM_STATE:
                         SSM_INNER + (SSM_GROUPS + g + 1) * SSM_STATE]
            scores.append(_dot_nt(cg.astype(BF16), bg.astype(BF16)))
            bm_t.append(bg.T)
            cms.append(cg)
        return cs, cs_t, dt_t, w_t, etot, scores, bm_t, cms

    def chunk(j, prep, after_pair):
        rows = slice(j * q, (j + 1) * q)
        cs, cs_t, dt_t, w_t, etot, scores, bm_t, cms = prep
        for pair in range(N_PAIRS):
            g = (2 * pair) // HEADS_PER_GROUP
            lanes = slice(pair * LANES, (pair + 1) * LANES)
            xs_p = xbc_scr[rows, lanes]
            ht = ht_scr[pair]
            xsb = xs_p.astype(BF16)
            rhs = jnp.concatenate([xsb, ht.astype(BF16)], axis=0)
            ys, upds = [], []
            for h in (2 * pair, 2 * pair + 1):
                col = jnp.broadcast_to(cs[:, h:h + 1], (q, q))
                row = jnp.broadcast_to(cs_t[h:h + 1, :], (q, q))
                lmat = jnp.exp2(jnp.where(tril, col - row, -jnp.inf))
                lhs = jnp.concatenate([(scores[g] * lmat * dt_t[h:h + 1, :]).astype(BF16),
                                       (cms[g] * jnp.exp2(col)).astype(BF16)], axis=1)
                ys.append(_dot(lhs, rhs))
                upds.append(_dot((bm_t[g] * w_t[h:h + 1, :]).astype(BF16), xsb))
            xbc_scr[rows, lanes] = (jnp.where(left, ys[0], ys[1])
                                    + vec_ref[:, _V_DSK + pair * LANES:_V_DSK + (pair + 1) * LANES] * xs_p)
            dec = jnp.where(left, jnp.broadcast_to(etot[:, 2 * pair:2 * pair + 1], (1, LANES)),
                            jnp.broadcast_to(etot[:, 2 * pair + 1:2 * pair + 2], (1, LANES)))
            ht_scr[pair] = ht * dec + jnp.where(left, upds[0], upds[1])
            after_pair(j * N_PAIRS + pair)

    jobs = [(w_ref, k, act) for w_ref, act in ((wsz0_ref, _silu), (wsz1_ref, _silu), (wgs_ref, _sigmoid))
            for k in range(WBLK // SIDE_W)]
    n_slots = (t // q) * N_PAIRS
    job_at = {((2 * i + 1) * n_slots) // (2 * len(jobs)): i for i in range(len(jobs))}
    assert len(job_at) == len(jobs)
    side_vals = []

    def after_pair(slot):
        if slot in job_at:
            w_ref, k, act = jobs[job_at[slot]]
            side_vals.append(act(_dot(ub, w_ref[:, k * SIDE_W:(k + 1) * SIDE_W])))

    preps = [chunk_prep(j) for j in range(t // q)]
    for j in range(t // q):
        chunk(j, preps[j], after_pair)
    per_blk = WBLK // SIDE_W
    zgate = [jnp.concatenate(side_vals[b * per_blk:(b + 1) * per_blk], axis=1) for b in range(2)]
    g_s = jnp.concatenate(side_vals[2 * per_blk:], axis=1)

    @pl.when(c == pl.num_programs(1) - 1)
    def _():
        for pair in range(N_PAIRS):
            ssm_ref[pair * LANES:(pair + 1) * LANES, :] = ht_scr[pair].T

    parts = []
    for b in range(SSM_INNER // WBLK):
        gz = xbc_scr[:, b * WBLK:(b + 1) * WBLK] * zgate[b]
        for g in range(WBLK // GROUP_W):
            gg = gz[:, g * GROUP_W:(g + 1) * GROUP_W]
            off = _V_SNW + b * WBLK + g * GROUP_W
            parts.append((gg * lax.rsqrt(jnp.mean(gg * gg, axis=-1, keepdims=True) + EPS)
                          * vec_ref[:, off:off + GROUP_W]).astype(BF16))
    y_s = _dot(jnp.concatenate(parts, axis=1), wso_ref[...])
    merged = acc_ref[...] + g_s * y_s
    xn = x_ref[...] + _dot(merged.astype(BF16), wo_ref[...])
    out_ref[...] = _rms(xn, _vec(vec_ref, _V_FNW, D_MODEL)) if final_norm else xn


def _prompt_ssd(x, acc, vec, w_all, wdt, csw, wso, wo, layer, tile, final_norm):
    bt, seq, _ = x.shape
    nc = seq // tile
    tok = lambda w: pl.BlockSpec((None, tile, w), lambda b, c: (b, c, 0))
    per_b = lambda r, w: pl.BlockSpec((None, r, w), lambda b, c: (b, 0, 0))
    blocks = (_B_SZ, _B_SZ + 1, _B_SX, _B_SX + 1, _B_SX + 2, _B_GS)
    return pl.pallas_call(
        functools.partial(_ssd_kernel, final_norm=final_norm),
        grid=(bt, nc),
        in_specs=([tok(D_MODEL), tok(D_MODEL), _vec_spec(layer)] + [_wblk(layer, b) for b in blocks]
                  + [_layer_mat((D_MODEL, LANES), layer), _layer_mat((4, SSM_XBC), layer),
                     _layer_mat((SSM_INNER, D_MODEL), layer), _layer_mat((D_MODEL, D_MODEL), layer)]),
        out_specs=[tok(D_MODEL), per_b(3, SSM_XBC), per_b(SSM_HEADS * SSM_HEADDIM, SSM_STATE)],
        out_shape=[jax.ShapeDtypeStruct((bt, seq, D_MODEL), F32),
                   jax.ShapeDtypeStruct((bt, 3, SSM_XBC), F32),
                   jax.ShapeDtypeStruct((bt, SSM_HEADS * SSM_HEADDIM, SSM_STATE), F32)],
        scratch_shapes=[pltpu.VMEM((STRIPS, tile + SUBLANES, LANES), F32),
                        pltpu.VMEM((SSM_XBC // LANES, SUBLANES, LANES), F32),
                        pltpu.VMEM((tile, SSM_XBC), F32),
                        pltpu.VMEM((tile, LANES), F32),
                        pltpu.VMEM((N_PAIRS, SSM_STATE, LANES), F32)],
        compiler_params=_params(("arbitrary", "arbitrary")),
        name="prompt_ssd",
    )(x, acc, vec, *_wargs(w_all, blocks), wdt, csw, wso, wo)


def _proj_kernel(x_ref, vec_ref, *refs):
    w_refs, o_ref = refs[:-1], refs[-1]
    j = pl.program_id(0)
    ub = _rms(x_ref[...], _vec(vec_ref, _V_NW, D_MODEL)).astype(BF16)
    bounds = _SRC_FIRST + (_N_BLK,)
    for src, w_ref in enumerate(w_refs):
        @pl.when((j >= bounds[src]) & (j < bounds[src + 1]))
        def _(w_ref=w_ref):
            o_ref[...] = _dot(ub, w_ref[...])


def _sample_proj(x, vec, w_all, layer):
    m = x.shape[0]
    bounds = _SRC_FIRST + (_N_BLK,)

    def wspec(src):
        first, count = bounds[src], bounds[src + 1] - bounds[src]
        return pl.BlockSpec((None, D_MODEL, WBLK), lambda j: (layer, 0, jnp.clip(j - first, 0, count - 1)))

    return pl.pallas_call(
        _proj_kernel,
        grid=(_N_BLK,),
        in_specs=[_resident((m, D_MODEL), (0, 0)), _vec_spec(layer)] + [wspec(s) for s in range(len(w_all))],
        out_specs=pl.BlockSpec((m, WBLK), lambda j: (0, j)),
        out_shape=jax.ShapeDtypeStruct((m, _S_END), F32),
        compiler_params=_params(("arbitrary",)),
        name="sample_proj",
    )(x, vec, *w_all)


def _pre_kernel(p_ref, sca_ref, scs_ref, vec_ref, caw_ref, csw_ref,
                acta_ref, row_ref, xs_ref, nca_ref, ncs_ref):
    cb = p_ref[:, _S_A:_S_A + CONV_W]
    v = p_ref[:, _S_A + CONV_W:_S_A + 2 * CONV_W] * p_ref[:, _S_A + 2 * CONV_W:_S_A + 3 * CONV_W]
    cz = p_ref[:, _S_A + 3 * CONV_W:_S_A + 4 * CONV_W]
    caw = caw_ref[...]
    b0 = sca_ref[:, 0:CONV_W]
    b1 = sca_ref[:, CONV_W:2 * CONV_W]
    conv = caw[0:1] * b0 + caw[1:2] * b1 + caw[2:3] * v
    acta_ref[...] = _silu(cz) * cb * conv
    nca_ref[:, 0:CONV_W] = b1
    nca_ref[:, CONV_W:2 * CONV_W] = v

    sxbc = p_ref[:, _S_SX:_S_SX + SSM_XBC]
    csw = csw_ref[...]
    s0 = scs_ref[0]
    s1 = scs_ref[1]
    s2 = scs_ref[2]
    xbc = _silu(csw[0:1] * s0 + csw[1:2] * s1 + csw[2:3] * s2 + csw[3:4] * sxbc + _vec(vec_ref, _V_CSB, SSM_XBC))
    ncs_ref[0] = s1
    ncs_ref[1] = s2
    ncs_ref[2] = sxbc
    xs = xbc[:, :SSM_INNER]
    dt = _softplus(p_ref[:, _S_DT:_S_DT + SSM_INNER] + _vec(vec_ref, _V_DTBE, SSM_INNER))
    xs_ref[...] = xs
    row_ref[:, 0:_R_DA] = xs * dt
    row_ref[:, _R_DA:_R_B] = jnp.exp(dt * -jnp.exp(_vec(vec_ref, _V_ALOGE, SSM_INNER)))
    row_ref[:, _R_B:_R_END] = xbc[:, SSM_INNER:]


def _sample_pre(proj, sca, scs_all, vec, caw, csw, layer, rb):
    m = proj.shape[0]
    taps = scs_all.shape[1]
    rows = lambda w: pl.BlockSpec((rb, w), lambda i: (i, 0))
    return pl.pallas_call(
        _pre_kernel,
        grid=(m // rb,),
        in_specs=[rows(_S_END), rows(2 * CONV_W),
                  pl.BlockSpec((None, taps, rb, SSM_XBC), lambda i: (layer, 0, i, 0)), _vec_spec(layer),
                  _layer_mat((3, CONV_W), layer), _layer_mat((4, SSM_XBC), layer)],
        out_specs=[rows(CONV_W), rows(_R_END), rows(SSM_INNER), rows(2 * CONV_W),
                   pl.BlockSpec((taps, rb, SSM_XBC), lambda i: (0, i, 0))],
        out_shape=[jax.ShapeDtypeStruct((m, CONV_W), F32),
                   jax.ShapeDtypeStruct((m, _R_END), F32),
                   jax.ShapeDtypeStruct((m, SSM_INNER), F32),
                   jax.ShapeDtypeStruct((m, 2 * CONV_W), F32),
                   jax.ShapeDtypeStruct((taps, m, SSM_XBC), F32)],
        compiler_params=_params(("arbitrary",)),
        name="sample_pre",
    )(proj, sca, scs_all, vec, caw, csw)


def _sssd_kernel(row_ref, h0_ref, *rest):
    hn_ref, y_ref = rest[-2:]
    nb = row_ref.shape[0]
    sub_x = lax.broadcasted_iota(jnp.int32, (LANES, GROUP_W), 0)
    sub_r = lax.broadcasted_iota(jnp.int32, (LANES, 2 * SSM_STATE), 0)
    eye = (lax.broadcasted_iota(jnp.int32, (LANES, SSM_STATE), 0)
           == lax.broadcasted_iota(jnp.int32, (LANES, SSM_STATE), 1))
    zero = jnp.zeros((1, SSM_STATE), F32)
    for i in range(nb):
        row = row_ref[i]
        xparts = []
        rk = jnp.zeros((LANES, 2 * SSM_STATE), F32)
        for g in range(SSM_GROUPS):
            xdt = row[:, g * GROUP_W:(g + 1) * GROUP_W]
            x_hi = xdt.astype(BF16).astype(F32)
            x_lo = xdt - x_hi
            da = row[:, _R_DA + g * GROUP_W:_R_DA + (g + 1) * GROUP_W]
            a_hi = da.astype(BF16).astype(F32)
            a_mid = (da - a_hi).astype(BF16).astype(F32)
            a_lo = da - a_hi - a_mid
            k = sub_x - g * SUBLANES
            xparts.append(jnp.where((k == 0) | (k == 2), x_hi,
                                    jnp.where((k == 1) | (k == 3), x_lo,
                                              jnp.where(k == 4, a_hi,
                                                        jnp.where(k == 5, a_mid,
                                                                  jnp.where(k == 6, a_lo, 0.0))))))
            bg = row[:, _R_B + g * SSM_STATE:_R_B + (g + 1) * SSM_STATE]
            b_hi = bg.astype(BF16).astype(F32)
            b_lo = bg - b_hi
            kr = sub_r - g * SUBLANES
            rk = jnp.where((kr == 0) | (kr == 1), jnp.concatenate([b_hi, zero], axis=1),
                           jnp.where((kr == 2) | (kr == 3), jnp.concatenate([b_lo, zero], axis=1),
                                     jnp.where((kr >= 4) & (kr < 7), jnp.concatenate([zero, zero + 1.0], axis=1),
                                               rk)))
        xk = jnp.concatenate(xparts, axis=1)
        bc = _dot(xk.T.astype(BF16), rk.astype(BF16))
        hnew = bc[:, SSM_STATE:] * h0_ref[i] + bc[:, :SSM_STATE]
        hn_ref[i] = hnew
        hb = hnew.astype(BF16)
        ys = []
        for g in range(SSM_GROUPS):
            cg = row[:, _R_C + g * SSM_STATE:_R_C + (g + 1) * SSM_STATE]
            c_cols = jnp.broadcast_to(cg, (SSM_STATE, SSM_STATE)).T.astype(BF16)
            rep = _dot(hb[g * GROUP_W:(g + 1) * GROUP_W, :], c_cols)
            for k in range(GROUP_W // LANES):
                ys.append(jnp.sum(jnp.where(eye, rep[k * LANES:(k + 1) * LANES, :], 0.0), axis=0, keepdims=True))
        y_ref[i] = jnp.concatenate(ys, axis=1)


def _sample_ssd(rows, state_all, layer, stacked, nb):
    depth, m, hp, _ = state_all.shape
    state_spec = pl.BlockSpec((None, nb, hp, SSM_STATE), lambda i: (layer, i, 0, 0))
    in_specs = [pl.BlockSpec((nb, 1, _R_END), lambda i: (i, 0, 0)), state_spec]
    args = [rows.reshape(m, 1, _R_END), state_all]
    aliases = {}
    if stacked is not None:
        in_specs.append(pl.BlockSpec(memory_space=pl.ANY))
        args.append(stacked)
        aliases = {2: 0}
    return pl.pallas_call(
        _sssd_kernel,
        grid=(m // nb,),
        in_specs=in_specs,
        out_specs=[state_spec, pl.BlockSpec((nb, 1, hp), lambda i: (i, 0, 0))],
        out_shape=[jax.ShapeDtypeStruct((depth, m, hp, SSM_STATE), F32),
                   jax.ShapeDtypeStruct((m, 1, hp), F32)],
        input_output_aliases=aliases,
        compiler_params=_params(("arbitrary",)),
        name="sample_ssd",
    )(*args)


def _sattn_kernel(q_ref, k_ref, v_ref, o_ref):
    nb = q_ref.shape[0]
    halves = MEM_HEADDIM // LANES
    rows = halves * MEM_HEADS
    row = lax.broadcasted_iota(jnp.int32, (rows, CACHE_ROWS), 0)
    col = lax.broadcasted_iota(jnp.int32, (rows, CACHE_ROWS), 1)
    own = (col & (rows - 1)) == row
    lane = lax.broadcasted_iota(jnp.int32, (1, LANES), 1)
    low_half = (lane & MEM_HEADS) == 0
    for i in range(nb):
        q = q_ref[i]
        q8 = jnp.concatenate([q[:, (h * halves + j) * LANES:(h * halves + j + 1) * LANES]
                              for j in range(halves) for h in range(MEM_HEADS)], axis=0)
        s = _dot_nt(q8.astype(BF16), k_ref[i].astype(BF16))
        t = jnp.sum(jnp.where(own, s, 0.0), axis=0, keepdims=True)
        parts = []
        for b in range(CACHE_ROWS // LANES):
            tb = t[:, b * LANES:(b + 1) * LANES]
            parts.append(tb + jnp.where(low_half, pltpu.roll(tb, LANES - MEM_HEADS, 1),
                                        pltpu.roll(tb, MEM_HEADS, 1)))
        u = jnp.concatenate(parts, axis=1) * (MEM_HEADDIM ** -0.5)
        p = _softmax_rows(jnp.where(own, u, -jnp.inf))
        o8 = _dot(p.astype(BF16), v_ref[i].astype(BF16))
        for j in range(halves):
            for h in range(MEM_HEADS):
                r = j * MEM_HEADS + h
                o_ref[i, :, (h * halves + j) * LANES:(h * halves + j + 1) * LANES] = o8[r:r + 1, :]


def _cache_rows(c):
    depth, m = c.shape[:2]
    c = c.reshape(depth, m, N_MEM, MEM_HEADS, MEM_HEADDIM // LANES, LANES)
    return c.transpose(0, 1, 2, 4, 3, 5).reshape(depth, m, CACHE_ROWS, LANES)


def _sample_attn(q, k_rows, v_rows, layer, nb):
    m = q.shape[0]
    cache = pl.BlockSpec((None, nb, CACHE_ROWS, LANES), lambda i: (layer, i, 0, 0))
    return pl.pallas_call(
        _sattn_kernel,
        grid=(m // nb,),
        in_specs=[pl.BlockSpec((nb, 1, ATTN_W), lambda i: (i, 0, 0)), cache, cache],
        out_specs=pl.BlockSpec((nb, 1, ATTN_W), lambda i: (i, 0, 0)),
        out_shape=jax.ShapeDtypeStruct((m, 1, ATTN_W), F32),
        compiler_params=_params(("arbitrary",)),
        name="sample_attn",
    )(q.reshape(m, 1, ATTN_W), k_rows, v_rows)


def _post_kernel(x_ref, p_ref, acta_ref, xs_ref, ysd_ref, o_ref, vec_ref, wao_ref, wso_ref, wmo_ref, wo_ref,
                 out_ref, *, final_norm):
    y_a = _dot(acta_ref[...].astype(BF16), wao_ref[...])
    gz = ((ysd_ref[...] + _vec(vec_ref, _V_DSK, SSM_INNER) * xs_ref[...])
          * _silu(p_ref[:, _S_SZ:_S_SZ + SSM_INNER]))
    parts = []
    for g in range(SSM_GROUPS):
        gg = gz[:, g * GROUP_W:(g + 1) * GROUP_W]
        parts.append(gg * lax.rsqrt(jnp.mean(gg * gg, axis=-1, keepdims=True) + EPS))
    gn = jnp.concatenate(parts, axis=1) * _vec(vec_ref, _V_SNW, SSM_INNER)
    y_s = _dot(gn.astype(BF16), wso_ref[...])
    mz = p_ref[:, _S_M + ATTN_W:_S_M + 2 * ATTN_W]
    y_m = _dot((_silu(mz) * o_ref[...]).astype(BF16), wmo_ref[...])
    merged = (_sigmoid(p_ref[:, _S_G:_S_G + D_MODEL]) * y_a
              + _sigmoid(p_ref[:, _S_G + D_MODEL:_S_G + 2 * D_MODEL]) * y_s
              + _sigmoid(p_ref[:, _S_G + 2 * D_MODEL:_S_G + 3 * D_MODEL]) * y_m)
    xn = x_ref[...] + _dot(merged.astype(BF16), wo_ref[...])
    out_ref[...] = _rms(xn, _vec(vec_ref, _V_FNW, D_MODEL)) if final_norm else xn


def _sample_post(x, proj, acta, xs, ysd, o, vec, wao, wso, wmo, wo, layer, final_norm):
    m = x.shape[0]
    full = lambda a: _resident(a.shape, (0,) * a.ndim)
    acts = (x, proj, acta, xs, ysd, o)
    return pl.pallas_call(
        functools.partial(_post_kernel, final_norm=final_norm),
        grid=(1,),
        in_specs=([full(a) for a in acts]
                  + [_vec_spec(layer), _layer_mat((CONV_W, D_MODEL), layer), _layer_mat((SSM_INNER, D_MODEL), layer),
                     _layer_mat((ATTN_W, D_MODEL), layer), _layer_mat((D_MODEL, D_MODEL), layer)]),
        out_specs=_resident((m, D_MODEL), (0, 0)),
        out_shape=jax.ShapeDtypeStruct((m, D_MODEL), F32),
        compiler_params=_params(("arbitrary",)),
        name="sample_post",
    )(*acts, vec, wao, wso, wmo, wo)


_TAIL_FIRST_A = _OFF_DT // WBLK
_TAIL_FIRST_B = _OFF_DT // LANES


def _regroup_kernel(a_ref, b_ref, last_ref, o_ref):
    j = pl.program_id(1)

    @pl.when(j < _B_DT)
    def _():
        o_ref[...] = a_ref[...].T.astype(BF16)

    @pl.when((j >= _B_DT) & (j < _B_Q))
    def _():
        heads = WBLK // SSM_HEADDIM
        first = (j - _B_DT) * heads
        rows = jnp.concatenate([jnp.broadcast_to(b_ref[pl.ds(first + k, 1), :], (SSM_HEADDIM, D_MODEL))
                                for k in range(heads)], axis=0)
        o_ref[...] = rows.T.astype(BF16)

    @pl.when(j >= _B_Q)
    def _():
        nxt = jnp.where(j == _N_BLK - 1, last_ref[...], b_ref[...])
        rows = jnp.concatenate([a_ref[SSM_HEADS:, :], nxt[:SSM_HEADS, :]], axis=0)
        o_ref[...] = rows.T.astype(BF16)


def _regroup_w_in(w_in):
    depth, _, n_in = w_in.shape
    w_in = w_in.transpose(0, 2, 1)
    last = jnp.pad(w_in[:, n_in - SSM_HEADS:, :], ((0, 0), (0, LANES - SSM_HEADS), (0, 0)))
    a_idx = lambda j: jnp.where(j < _B_DT, j, jnp.clip(j - _B_Q + _TAIL_FIRST_A, _TAIL_FIRST_A, n_in // WBLK - 1))
    b_idx = lambda j: jnp.where(j < _B_Q, _TAIL_FIRST_B,
                                jnp.minimum(_TAIL_FIRST_B + (WBLK // LANES) * (j - _B_Q + 1), n_in // LANES - 1))
    return pl.pallas_call(
        _regroup_kernel,
        grid=(depth, _N_BLK),
        in_specs=[pl.BlockSpec((None, WBLK, D_MODEL), lambda l, j: (l, a_idx(j), 0)),
                  pl.BlockSpec((None, LANES, D_MODEL), lambda l, j: (l, b_idx(j), 0)),
                  pl.BlockSpec((None, LANES, D_MODEL), lambda l, j: (l, 0, 0))],
        out_specs=pl.BlockSpec((None, D_MODEL, WBLK), lambda l, j: (l, 0, j)),
        out_shape=jax.ShapeDtypeStruct((depth, D_MODEL, _S_END), BF16),
        compiler_params=_params(("arbitrary", "arbitrary")),
        name="regroup_w_in",
    )(w_in, w_in, last)


def _pad_lanes(a):
    return jnp.pad(a, ((0, 0), (0, LANES - a.shape[1])))


def _prompt_tile(seq):
    for t in (512, 256, 128):
        if seq % t == 0:
            return t
    raise ValueError(f"sequence length {seq} must be a multiple of {SSD_CHUNK}")


def kernel(x_prompt, x_sample, mem_prompt, state_conv_a, state_conv_s, state_ssm, cache_mem_k, cache_mem_v,
           norm_w, w_in, conv_a_w, w_a_out, conv_s_w, conv_s_b, dt_bias, a_log, d_skip, ssm_norm_w,
           w_s_out, mem_norm_w, w_mem_kv, w_m_out, w_o, final_norm_w):
    depth = w_in.shape[0]
    bp, seq, _ = x_prompt.shape
    bs = x_sample.shape[0]
    assert x_sample.shape[1] == 1
    tile = _prompt_tile(seq)

    w_all = (_regroup_w_in(w_in),)
    wdt = jnp.pad(w_in[:, :, _OFF_DT:_OFF_M].astype(BF16), ((0, 0), (0, 0), (0, LANES - SSM_HEADS)))
    wao, wso, wmo, wo, wkv = (a.astype(BF16) for a in (w_a_out, w_s_out, w_m_out, w_o, w_mem_kv))
    rep = lambda a: jnp.repeat(a, SSM_HEADDIM, axis=1)
    vec = jnp.concatenate([norm_w, mem_norm_w, conv_s_b, _pad_lanes(dt_bias), _pad_lanes(a_log), rep(d_skip),
                           ssm_norm_w, rep(dt_bias), rep(a_log),
                           jnp.broadcast_to(final_norm_w[None], (depth, D_MODEL))], axis=1)[:, None, :]

    x = x_prompt
    pa, ps, ph = [], [], []
    pk_rows, pv_rows, kt, vb = _mem_kv(mem_prompt, vec, wkv)
    for l in range(depth):
        last = l == depth - 1
        acc, ca = _prompt_am(x, vec, w_all, conv_a_w, wao, wmo, kt, vb, l, tile)
        x, cs, hs = _prompt_ssd(x, acc, vec, w_all, wdt, conv_s_w, wso, wo, l, tile, last)
        pa.append(ca)
        ps.append(cs)
        ph.append(hs.reshape(bp, SSM_HEADS, SSM_HEADDIM, SSM_STATE))
    y_prompt = x

    x = x_sample.reshape(bs, D_MODEL)
    state_all = state_ssm.reshape(depth, bs, SSM_HEADS * SSM_HEADDIM, SSM_STATE)
    k_rows, v_rows = _cache_rows(cache_mem_k), _cache_rows(cache_mem_v)
    scs_all = state_conv_s.transpose(0, 2, 1, 3)
    sa, ss, s_ssm = [], [], None
    for l in range(depth):
        last = l == depth - 1
        proj = _sample_proj(x, vec, w_all, l)
        acta, rows, xs, nca, ncs = _sample_pre(
            proj, state_conv_a[l].reshape(bs, 2 * CONV_W), scs_all, vec, conv_a_w, conv_s_w, l, 32)
        s_ssm, ysd = _sample_ssd(rows, state_all, l, s_ssm, SSD_SEQS_PER_STEP)
        o = _sample_attn(proj[:, _S_M:_S_M + ATTN_W], k_rows, v_rows, l, ATTN_SEQS_PER_STEP)
        x = _sample_post(x, proj, acta, xs, ysd.reshape(bs, SSM_INNER), o.reshape(bs, ATTN_W), vec,
                         wao, wso, wmo, wo, l, last)
        sa.append(nca.reshape(bs, 2, CONV_W))
        ss.append(ncs)
    y_sample = x.reshape(bs, 1, D_MODEL)

    return (y_prompt, y_sample, jnp.stack(pa), jnp.stack(ps), jnp.stack(ph),
            _cache_from_rows(pk_rows), _cache_from_rows(pv_rows),
            jnp.stack(sa), jnp.stack(ss).transpose(0, 2, 1, 3),
            s_ssm.reshape(depth, bs, SSM_HEADS, SSM_HEADDIM, SSM_STATE))
```

```python
import functools

import jax
import jax.numpy as jnp
from jax import lax
from jax.experimental import pallas as pl
from jax.experimental.pallas import tpu as pltpu

F32 = jnp.float32
BF16 = jnp.bfloat16

D_MODEL = 1024
CONV_W = 1024
SSM_INNER = 2048
SSM_HEADDIM = 64
SSM_HEADS = 32
SSM_GROUPS = 4
SSM_STATE = 128
SSM_XBC = SSM_INNER + 2 * SSM_GROUPS * SSM_STATE
HEADS_PER_GROUP = SSM_HEADS // SSM_GROUPS
GROUP_W = SSM_INNER // SSM_GROUPS
N_PAIRS = SSM_HEADS // 2
N_MEM = 256
MEM_HEADS = 4
MEM_HEADDIM = 256
ATTN_W = MEM_HEADS * MEM_HEADDIM
EPS = 1e-6
LOG2E = 1.4426950408889634
SSD_CHUNK = 128
LANES = 128
SUBLANES = 8
CACHE_ROWS = N_MEM * ATTN_W // LANES
VMEM_LIMIT = 58 * 1024 * 1024
WBLK = 1024
SIDE_W = 256
SSD_SEQS_PER_STEP = 8
ATTN_SEQS_PER_STEP = 8
STRIPS = WBLK // LANES

_B_CB, _B_CC, _B_CH, _B_CZ = 0, 1, 2, 3
_B_SZ = 4
_B_SX = 6
_B_DT = 9
_B_Q, _B_MZ = 11, 12
_B_GA, _B_GS, _B_GM = 13, 14, 15
_N_BLK = 16
_S_A = 0
_S_SZ = _B_SZ * WBLK
_S_SX = _B_SX * WBLK
_S_DT = _B_DT * WBLK
_S_M = _B_Q * WBLK
_S_G = _B_GA * WBLK
_S_END = _N_BLK * WBLK
_OFF_DT = 4 * CONV_W + SSM_INNER + SSM_XBC
_OFF_M = _OFF_DT + SSM_HEADS
_R_DA = SSM_INNER
_R_B = 2 * SSM_INNER
_R_C = _R_B + SSM_GROUPS * SSM_STATE
_R_END = _R_C + SSM_GROUPS * SSM_STATE

_V_NW = 0
_V_MNW = _V_NW + D_MODEL
_V_CSB = _V_MNW + D_MODEL
_V_DTB = _V_CSB + SSM_XBC
_V_ALOG = _V_DTB + LANES
_V_DSK = _V_ALOG + LANES
_V_SNW = _V_DSK + SSM_INNER
_V_DTBE = _V_SNW + SSM_INNER
_V_ALOGE = _V_DTBE + SSM_INNER
_V_FNW = _V_ALOGE + SSM_INNER
_V_END = _V_FNW + D_MODEL


def _dot(a, b):
    return jnp.dot(a, b, preferred_element_type=F32)


def _dot_nt(a, b):
    return lax.dot_general(a, b, (((1,), (1,)), ((), ())), preferred_element_type=F32)


def _rms(x, w):
    return x * lax.rsqrt(jnp.mean(x * x, axis=-1, keepdims=True) + EPS) * w


def _sigmoid(x):
    return 0.5 * jnp.tanh(0.5 * x) + 0.5


def _silu(x):
    h = 0.5 * x
    return h * jnp.tanh(h) + h


def _softplus(x):
    return jnp.maximum(x, 0.0) + jnp.log1p(jnp.exp(-jnp.abs(x)))


def _softmax_rows(s):
    m = jnp.max(s, axis=-1, keepdims=True)
    e = jnp.exp(s - m)
    return e / jnp.sum(e, axis=-1, keepdims=True)


def _split3(x):
    hi = x.astype(BF16)
    r1 = x - hi.astype(F32)
    mid = r1.astype(BF16)
    lo = (r1 - mid.astype(F32)).astype(BF16)
    return hi, mid, lo


def _vec(vec_ref, off, width):
    return vec_ref[:, off:off + width]


def _resident(block, index):
    return pl.BlockSpec(block, lambda *_: index, pipeline_mode=pl.Buffered(1))


_SRC_FIRST = (0,)


def _wsrc(blk):
    src = max(i for i, first in enumerate(_SRC_FIRST) if blk >= first)
    return src, blk - _SRC_FIRST[src]


def _wblk(layer, blk):
    return _resident((None, D_MODEL, WBLK), (layer, 0, _wsrc(blk)[1]))


def _wargs(wsrc, blocks):
    return [wsrc[_wsrc(b)[0]] for b in blocks]


def _vec_spec(layer):
    return _resident((None, 1, _V_END), (layer, 0, 0))


def _layer_mat(shape, layer):
    return _resident((None,) + shape, (layer, 0, 0))


def _params(sem):
    return pltpu.CompilerParams(dimension_semantics=sem, vmem_limit_bytes=VMEM_LIMIT)


def _kv_kernel(mem_ref, vec_ref, w_ref, k_ref, v_ref, kt_ref, vb_ref):
    u = _rms(mem_ref[...], _vec(vec_ref, _V_MNW, D_MODEL)).astype(BF16)
    kv = _dot(u, w_ref[...])
    k = kv[:, :ATTN_W]
    v = kv[:, ATTN_W:]
    halves = MEM_HEADDIM // LANES
    for j in range(halves):
        for h in range(MEM_HEADS):
            rows = pl.ds(j * MEM_HEADS + h, N_MEM, stride=halves * MEM_HEADS)
            sl = slice((h * halves + j) * LANES, (h * halves + j + 1) * LANES)
            k_ref[rows, :] = k[:, sl]
            v_ref[rows, :] = v[:, sl]
    kt_ref[...] = (k.T * (MEM_HEADDIM ** -0.5)).astype(BF16)
    vb_ref[...] = v.astype(BF16)


def _mem_kv(mem, vec, w_kv):
    depth = w_kv.shape[0]
    bt = mem.shape[0]
    out = lambda r, c: pl.BlockSpec((None, None, r, c), lambda l, b: (l, b, 0, 0))
    per_layer = lambda r, c: pl.BlockSpec((None, r, c), lambda l, b: (l, 0, 0))
    return pl.pallas_call(
        _kv_kernel,
        grid=(depth, bt),
        in_specs=[pl.BlockSpec((None, N_MEM, D_MODEL), lambda l, b: (b, 0, 0)), per_layer(1, _V_END),
                  per_layer(D_MODEL, 2 * ATTN_W)],
        out_specs=[out(CACHE_ROWS, LANES), out(CACHE_ROWS, LANES), out(ATTN_W, N_MEM), out(N_MEM, ATTN_W)],
        out_shape=[jax.ShapeDtypeStruct((depth, bt, CACHE_ROWS, LANES), F32),
                   jax.ShapeDtypeStruct((depth, bt, CACHE_ROWS, LANES), F32),
                   jax.ShapeDtypeStruct((depth, bt, ATTN_W, N_MEM), BF16),
                   jax.ShapeDtypeStruct((depth, bt, N_MEM, ATTN_W), BF16)],
        compiler_params=_params(("arbitrary", "arbitrary")),
        name="prompt_kv",
    )(mem, vec, w_kv)


def _cache_from_rows(c):
    depth, m = c.shape[:2]
    c = c.reshape(depth, m, N_MEM, MEM_HEADDIM // LANES, MEM_HEADS, LANES)
    return c.transpose(0, 1, 2, 4, 3, 5).reshape(depth, m, N_MEM, MEM_HEADS, MEM_HEADDIM)


def _am_kernel(x_ref, vec_ref, wcb_ref, wcc_ref, wch_ref, wcz_ref, wq_ref, wmz_ref, wga_ref, wgm_ref,
               caw_ref, wao_ref, wmo_ref, kt_ref, vb_ref,
               acc_ref, ca_ref, vbuf):
    c = pl.program_id(1)
    t = x_ref.shape[0]
    pad = SUBLANES

    @pl.when(c == 0)
    def _():
        vbuf[:, 0:pad, :] = jnp.zeros((STRIPS, pad, LANES), F32)

    ub = _rms(x_ref[...], _vec(vec_ref, _V_NW, D_MODEL)).astype(BF16)

    v = _dot(ub, wcc_ref[...]) * _dot(ub, wch_ref[...])
    gate = _silu(_dot(ub, wcz_ref[...])) * _dot(ub, wcb_ref[...])
    caw = caw_ref[...]
    acts = []
    for j in range(STRIPS):
        sl = slice(j * LANES, (j + 1) * LANES)
        vj = v[:, sl]
        vbuf[j, pad:pad + t, :] = vj
        conv = (caw[0:1, sl] * vbuf[j, pad - 2:pad - 2 + t, :] + caw[1:2, sl] * vbuf[j, pad - 1:pad - 1 + t, :]
                + caw[2:3, sl] * vj)
        acts.append((gate[:, sl] * conv).astype(BF16))
        ca_ref[:, sl] = vbuf[j, pad + t - 2:pad + t, :]
        vbuf[j, 0:pad, :] = vbuf[j, t:t + pad, :]
    y_a = _dot(jnp.concatenate(acts, axis=1), wao_ref[...])

    q = _dot(ub, wq_ref[...])
    mz = _dot(ub, wmz_ref[...])
    acts = []
    for h in range(MEM_HEADS):
        sl = slice(h * MEM_HEADDIM, (h + 1) * MEM_HEADDIM)
        p = _softmax_rows(_dot(q[:, sl].astype(BF16), kt_ref[sl, :]))
        o = _dot(p.astype(BF16), vb_ref[:, sl])
        acts.append((_silu(mz[:, sl]) * o).astype(BF16))
    y_m = _dot(jnp.concatenate(acts, axis=1), wmo_ref[...])

    acc_ref[...] = _sigmoid(_dot(ub, wga_ref[...])) * y_a + _sigmoid(_dot(ub, wgm_ref[...])) * y_m


def _prompt_am(x, vec, w_all, caw, wao, wmo, kt, vb, layer, tile):
    bt, seq, _ = x.shape
    nc = seq // tile
    tok = lambda w: pl.BlockSpec((None, tile, w), lambda b, c: (b, c, 0))
    per_b = lambda r, w: pl.BlockSpec((None, r, w), lambda b, c: (b, 0, 0))
    mem_kv = lambda r, w: pl.BlockSpec((None, None, r, w), lambda b, c: (layer, b, 0, 0))
    blocks = (_B_CB, _B_CC, _B_CH, _B_CZ, _B_Q, _B_MZ, _B_GA, _B_GM)
    return pl.pallas_call(
        _am_kernel,
        grid=(bt, nc),
        in_specs=([tok(D_MODEL), _vec_spec(layer)] + [_wblk(layer, b) for b in blocks]
                  + [_layer_mat((3, CONV_W), layer), _layer_mat((CONV_W, D_MODEL), layer),
                     _layer_mat((ATTN_W, D_MODEL), layer), mem_kv(ATTN_W, N_MEM), mem_kv(N_MEM, ATTN_W)]),
        out_specs=[tok(D_MODEL), per_b(2, CONV_W)],
        out_shape=[jax.ShapeDtypeStruct((bt, seq, D_MODEL), F32),
                   jax.ShapeDtypeStruct((bt, 2, CONV_W), F32)],
        scratch_shapes=[pltpu.VMEM((STRIPS, tile + SUBLANES, LANES), F32)],
        compiler_params=_params(("arbitrary", "arbitrary")),
        name="prompt_am",
    )(x, vec, *_wargs(w_all, blocks), caw, wao, wmo, kt, vb)


def _cumsum_rows(da):
    q = da.shape[0]
    r = lax.broadcasted_iota(jnp.int32, (q, q), 0)
    c = lax.broadcasted_iota(jnp.int32, (q, q), 1)
    tri = jnp.where(r >= c, 1.0, 0.0).astype(BF16)
    hi, mid, lo = _split3(da)
    return _dot(tri, hi) + _dot(tri, mid) + _dot(tri, lo)


def _ssd_kernel(x_ref, acc_ref, vec_ref, wsz0_ref, wsz1_ref, wsx0_ref, wsx1_ref, wsx2_ref, wgs_ref, wdt_ref,
                csw_ref, wso_ref, wo_ref,
                out_ref, cs_ref, ssm_ref,
                xbuf, tails, xbc_scr, dt_scr, ht_scr, *, final_norm):
    c = pl.program_id(1)
    t = x_ref.shape[0]
    q = SSD_CHUNK
    pad = SUBLANES

    @pl.when(c == 0)
    def _():
        tails[...] = jnp.zeros(tails.shape, F32)
        ht_scr[...] = jnp.zeros(ht_scr.shape, F32)

    ub = _rms(x_ref[...], _vec(vec_ref, _V_NW, D_MODEL)).astype(BF16)
    dt_scr[...] = _softplus(_dot(ub, wdt_ref[...]) + _vec(vec_ref, _V_DTB, LANES))

    csw = csw_ref[...]

    def conv_block(b, w_ref):
        sx = _dot(ub, w_ref[...])
        for j in range(STRIPS):
            s = b * STRIPS + j
            sl = slice(s * LANES, (s + 1) * LANES)
            sxj = sx[:, j * LANES:(j + 1) * LANES]
            xbuf[j, 0:pad, :] = tails[s]
            xbuf[j, pad:pad + t, :] = sxj
            yc = (csw[0:1, sl] * xbuf[j, pad - 3:pad - 3 + t, :] + csw[1:2, sl] * xbuf[j, pad - 2:pad - 2 + t, :]
                  + csw[2:3, sl] * xbuf[j, pad - 1:pad - 1 + t, :] + csw[3:4, sl] * sxj
                  + vec_ref[:, _V_CSB + s * LANES:_V_CSB + (s + 1) * LANES])
            xbc_scr[:, sl] = _silu(yc)
            cs_ref[:, sl] = xbuf[j, pad + t - 3:pad + t, :]
            tails[s] = xbuf[j, t:t + pad, :]

    a_row = -jnp.exp(_vec(vec_ref, _V_ALOG, LANES))

    rr = lax.broadcasted_iota(jnp.int32, (q, q), 0)
    cc = lax.broadcasted_iota(jnp.int32, (q, q), 1)
    tril = rr >= cc
    left = lax.broadcasted_iota(jnp.int32, (1, LANES), 1) < SSM_HEADDIM

    def chunk_prep(j):
        rows = slice(j * q, (j + 1) * q)
        dtj = dt_scr[rows, :]
        cs = _cumsum_rows(dtj * a_row) * LOG2E
        cs_t = cs.T
        dt_t = dtj.T
        w_t = dt_t * jnp.exp2(cs_t[:, q - 1:q] - cs_t)
        etot = jnp.exp2(cs[q - 1:q, :])

        scores, bm_t, cms = [], [], []
        for g in range(SSM_GROUPS):
            bg = xbc_scr[rows, SSM_INNER + g * SSM_STATE:SSM_INNER + (g + 1) * SSM_STATE]
            cg = xbc_scr[rows, SSM_INNER + (SSM_GROUPS + g) * SSM_STATE:
                         SSM_INNER + (SSM_GROUPS + g + 1) * SSM_STATE]
            scores.append(_dot_nt(cg.astype(BF16), bg.astype(BF16)))
            bm_t.append(bg.T)
            cms.append(cg)
        return cs, cs_t, dt_t, w_t, etot, scores, bm_t, cms

    def chunk(j, prep, after_pair):
        rows = slice(j * q, (j + 1) * q)
        cs, cs_t, dt_t, w_t, etot, scores, bm_t, cms = prep
        for pair in range(N_PAIRS):
            g = (2 * pair) // HEADS_PER_GROUP
            lanes = slice(pair * LANES, (pair + 1) * LANES)
            xs_p = xbc_scr[rows, lanes]
            ht = ht_scr[pair]
            xsb = xs_p.astype(BF16)
            rhs = jnp.concatenate([xsb, ht.astype(BF16)], axis=0)
            ys, upds = [], []
            for h in (2 * pair, 2 * pair + 1):
                col = jnp.broadcast_to(cs[:, h:h + 1], (q, q))
                row = jnp.broadcast_to(cs_t[h:h + 1, :], (q, q))
                lmat = jnp.exp2(jnp.where(tril, col - row, -jnp.inf))
                lhs = jnp.concatenate([(scores[g] * lmat * dt_t[h:h + 1, :]).astype(BF16),
                                       (cms[g] * jnp.exp2(col)).astype(BF16)], axis=1)
                ys.append(_dot(lhs, rhs))
                upds.append(_dot((bm_t[g] * w_t[h:h + 1, :]).astype(BF16), xsb))
            xbc_scr[rows, lanes] = (jnp.where(left, ys[0], ys[1])
                                    + vec_ref[:, _V_DSK + pair * LANES:_V_DSK + (pair + 1) * LANES] * xs_p)
            dec = jnp.where(left, jnp.broadcast_to(etot[:, 2 * pair:2 * pair + 1], (1, LANES)),
                            jnp.broadcast_to(etot[:, 2 * pair + 1:2 * pair + 2], (1, LANES)))
            ht_scr[pair] = ht * dec + jnp.where(left, upds[0], upds[1])
            after_pair(j * N_PAIRS + pair)

    jobs = [(w_ref, k, act) for w_ref, act in ((wsz0_ref, _silu), (wsz1_ref, _silu), (wgs_ref, _sigmoid))
            for k in range(WBLK // SIDE_W)]
    n_slots = (t // q) * N_PAIRS
    job_at = {((2 * i + 1) * n_slots) // (2 * len(jobs)): i for i in range(len(jobs))}
    assert len(job_at) == len(jobs)
    side_vals = []

    def after_pair(slot):
        if slot in job_at:
            w_ref, k, act = jobs[job_at[slot]]
            side_vals.append(act(_dot(ub, w_ref[:, k * SIDE_W:(k + 1) * SIDE_W])))

    for b, w_ref in enumerate((wsx0_ref, wsx1_ref, wsx2_ref)):
        conv_block(b, w_ref)
    preps = [chunk_prep(j) for j in range(t // q)]
    for j in range(t // q):
        chunk(j, preps[j], after_pair)
    per_blk = WBLK // SIDE_W
    zgate = [jnp.concatenate(side_vals[b * per_blk:(b + 1) * per_blk], axis=1) for b in range(2)]
    g_s = jnp.concatenate(side_vals[2 * per_blk:], axis=1)

    @pl.when(c == pl.num_programs(1) - 1)
    def _():
        for pair in range(N_PAIRS):
            ssm_ref[pair * LANES:(pair + 1) * LANES, :] = ht_scr[pair].T

    parts = []
    for b in range(SSM_INNER // WBLK):
        gz = xbc_scr[:, b * WBLK:(b + 1) * WBLK] * zgate[b]
        for g in range(WBLK // GROUP_W):
            gg = gz[:, g * GROUP_W:(g + 1) * GROUP_W]
            off = _V_SNW + b * WBLK + g * GROUP_W
            parts.append((gg * lax.rsqrt(jnp.mean(gg * gg, axis=-1, keepdims=True) + EPS)
                          * vec_ref[:, off:off + GROUP_W]).astype(BF16))
    y_s = _dot(jnp.concatenate(parts, axis=1), wso_ref[...])
    merged = acc_ref[...] + g_s * y_s
    xn = x_ref[...] + _dot(merged.astype(BF16), wo_ref[...])
    out_ref[...] = _rms(xn, _vec(vec_ref, _V_FNW, D_MODEL)) if final_norm else xn


def _prompt_ssd(x, acc, vec, w_all, wdt, csw, wso, wo, layer, tile, final_norm):
    bt, seq, _ = x.shape
    nc = seq // tile
    tok = lambda w: pl.BlockSpec((None, tile, w), lambda b, c: (b, c, 0))
    per_b = lambda r, w: pl.BlockSpec((None, r, w), lambda b, c: (b, 0, 0))
    blocks = (_B_SZ, _B_SZ + 1, _B_SX, _B_SX + 1, _B_SX + 2, _B_GS)
    return pl.pallas_call(
        functools.partial(_ssd_kernel, final_norm=final_norm),
        grid=(bt, nc),
        in_specs=([tok(D_MODEL), tok(D_MODEL), _vec_spec(layer)] + [_wblk(layer, b) for b in blocks]
                  + [_layer_mat((D_MODEL, LANES), layer), _layer_mat((4, SSM_XBC), layer),
                     _layer_mat((SSM_INNER, D_MODEL), layer), _layer_mat((D_MODEL, D_MODEL), layer)]),
        out_specs=[tok(D_MODEL), per_b(3, SSM_XBC), per_b(SSM_HEADS * SSM_HEADDIM, SSM_STATE)],
        out_shape=[jax.ShapeDtypeStruct((bt, seq, D_MODEL), F32),
                   jax.ShapeDtypeStruct((bt, 3, SSM_XBC), F32),
                   jax.ShapeDtypeStruct((bt, SSM_HEADS * SSM_HEADDIM, SSM_STATE), F32)],
        scratch_shapes=[pltpu.VMEM((STRIPS, tile + SUBLANES, LANES), F32),
                        pltpu.VMEM((SSM_XBC // LANES, SUBLANES, LANES), F32),
                        pltpu.VMEM((tile, SSM_XBC), F32),
                        pltpu.VMEM((tile, LANES), F32),
                        pltpu.VMEM((N_PAIRS, SSM_STATE, LANES), F32)],
        compiler_params=_params(("arbitrary", "arbitrary")),
        name="prompt_ssd",
    )(x, acc, vec, *_wargs(w_all, blocks), wdt, csw, wso, wo)


def _proj_kernel(x_ref, vec_ref, *refs):
    w_refs, o_ref = refs[:-1], refs[-1]
    j = pl.program_id(0)
    ub = _rms(x_ref[...], _vec(vec_ref, _V_NW, D_MODEL)).astype(BF16)
    bounds = _SRC_FIRST + (_N_BLK,)
    for src, w_ref in enumerate(w_refs):
        @pl.when((j >= bounds[src]) & (j < bounds[src + 1]))
        def _(w_ref=w_ref):
            o_ref[...] = _dot(ub, w_ref[...])


def _sample_proj(x, vec, w_all, layer):
    m = x.shape[0]
    bounds = _SRC_FIRST + (_N_BLK,)

    def wspec(src):
        first, count = bounds[src], bounds[src + 1] - bounds[src]
        return pl.BlockSpec((None, D_MODEL, WBLK), lambda j: (layer, 0, jnp.clip(j - first, 0, count - 1)))

    return pl.pallas_call(
        _proj_kernel,
        grid=(_N_BLK,),
        in_specs=[_resident((m, D_MODEL), (0, 0)), _vec_spec(layer)] + [wspec(s) for s in range(len(w_all))],
        out_specs=pl.BlockSpec((m, WBLK), lambda j: (0, j)),
        out_shape=jax.ShapeDtypeStruct((m, _S_END), F32),
        compiler_params=_params(("arbitrary",)),
        name="sample_proj",
    )(x, vec, *w_all)


def _pre_kernel(p_ref, sca_ref, scs_ref, vec_ref, caw_ref, csw_ref,
                acta_ref, row_ref, xs_ref, nca_ref, ncs_ref):
    cb = p_ref[:, _S_A:_S_A + CONV_W]
    v = p_ref[:, _S_A + CONV_W:_S_A + 2 * CONV_W] * p_ref[:, _S_A + 2 * CONV_W:_S_A + 3 * CONV_W]
    cz = p_ref[:, _S_A + 3 * CONV_W:_S_A + 4 * CONV_W]
    caw = caw_ref[...]
    b0 = sca_ref[:, 0:CONV_W]
    b1 = sca_ref[:, CONV_W:2 * CONV_W]
    conv = caw[0:1] * b0 + caw[1:2] * b1 + caw[2:3] * v
    acta_ref[...] = _silu(cz) * cb * conv
    nca_ref[:, 0:CONV_W] = b1
    nca_ref[:, CONV_W:2 * CONV_W] = v

    sxbc = p_ref[:, _S_SX:_S_SX + SSM_XBC]
    csw = csw_ref[...]
    s0 = scs_ref[0]
    s1 = scs_ref[1]
    s2 = scs_ref[2]
    xbc = _silu(csw[0:1] * s0 + csw[1:2] * s1 + csw[2:3] * s2 + csw[3:4] * sxbc + _vec(vec_ref, _V_CSB, SSM_XBC))
    ncs_ref[0] = s1
    ncs_ref[1] = s2
    ncs_ref[2] = sxbc
    xs = xbc[:, :SSM_INNER]
    dt = _softplus(p_ref[:, _S_DT:_S_DT + SSM_INNER] + _vec(vec_ref, _V_DTBE, SSM_INNER))
    xs_ref[...] = xs
    row_ref[:, 0:_R_DA] = xs * dt
    row_ref[:, _R_DA:_R_B] = jnp.exp(dt * -jnp.exp(_vec(vec_ref, _V_ALOGE, SSM_INNER)))
    row_ref[:, _R_B:_R_END] = xbc[:, SSM_INNER:]


def _sample_pre(proj, sca, scs_all, vec, caw, csw, layer, rb):
    m = proj.shape[0]
    taps = scs_all.shape[1]
    rows = lambda w: pl.BlockSpec((rb, w), lambda i: (i, 0))
    return pl.pallas_call(
        _pre_kernel,
        grid=(m // rb,),
        in_specs=[rows(_S_END), rows(2 * CONV_W),
                  pl.BlockSpec((None, taps, rb, SSM_XBC), lambda i: (layer, 0, i, 0)), _vec_spec(layer),
                  _layer_mat((3, CONV_W), layer), _layer_mat((4, SSM_XBC), layer)],
        out_specs=[rows(CONV_W), rows(_R_END), rows(SSM_INNER), rows(2 * CONV_W),
                   pl.BlockSpec((taps, rb, SSM_XBC), lambda i: (0, i, 0))],
        out_shape=[jax.ShapeDtypeStruct((m, CONV_W), F32),
                   jax.ShapeDtypeStruct((m, _R_END), F32),
                   jax.ShapeDtypeStruct((m, SSM_INNER), F32),
                   jax.ShapeDtypeStruct((m, 2 * CONV_W), F32),
                   jax.ShapeDtypeStruct((taps, m, SSM_XBC), F32)],
        compiler_params=_params(("arbitrary",)),
        name="sample_pre",
    )(proj, sca, scs_all, vec, caw, csw)


def _sssd_kernel(row_ref, h0_ref, *rest):
    hn_ref, y_ref = rest[-2:]
    nb = row_ref.shape[0]
    sub_x = lax.broadcasted_iota(jnp.int32, (LANES, GROUP_W), 0)
    sub_r = lax.broadcasted_iota(jnp.int32, (LANES, 2 * SSM_STATE), 0)
    eye = (lax.broadcasted_iota(jnp.int32, (LANES, SSM_STATE), 0)
           == lax.broadcasted_iota(jnp.int32, (LANES, SSM_STATE), 1))
    zero = jnp.zeros((1, SSM_STATE), F32)
    for i in range(nb):
        row = row_ref[i]
        xparts = []
        rk = jnp.zeros((LANES, 2 * SSM_STATE), F32)
        for g in range(SSM_GROUPS):
            xdt = row[:, g * GROUP_W:(g + 1) * GROUP_W]
            x_hi = xdt.astype(BF16).astype(F32)
            x_lo = xdt - x_hi
            da = row[:, _R_DA + g * GROUP_W:_R_DA + (g + 1) * GROUP_W]
            a_hi = da.astype(BF16).astype(F32)
            a_mid = (da - a_hi).astype(BF16).astype(F32)
            a_lo = da - a_hi - a_mid
            k = sub_x - g * SUBLANES
            xparts.append(jnp.where((k == 0) | (k == 2), x_hi,
                                    jnp.where((k == 1) | (k == 3), x_lo,
                                              jnp.where(k == 4, a_hi,
                                                        jnp.where(k == 5, a_mid,
                                                                  jnp.where(k == 6, a_lo, 0.0))))))
            bg = row[:, _R_B + g * SSM_STATE:_R_B + (g + 1) * SSM_STATE]
            b_hi = bg.astype(BF16).astype(F32)
            b_lo = bg - b_hi
            kr = sub_r - g * SUBLANES
            rk = jnp.where((kr == 0) | (kr == 1), jnp.concatenate([b_hi, zero], axis=1),
                           jnp.where((kr == 2) | (kr == 3), jnp.concatenate([b_lo, zero], axis=1),
                                     jnp.where((kr >= 4) & (kr < 7), jnp.concatenate([zero, zero + 1.0], axis=1),
                                               rk)))
        xk = jnp.concatenate(xparts, axis=1)
        bc = _dot(xk.T.astype(BF16), rk.astype(BF16))
        hnew = bc[:, SSM_STATE:] * h0_ref[i] + bc[:, :SSM_STATE]
        hn_ref[i] = hnew
        hb = hnew.astype(BF16)
        ys = []
        for g in range(SSM_GROUPS):
            cg = row[:, _R_C + g * SSM_STATE:_R_C + (g + 1) * SSM_STATE]
            c_cols = jnp.broadcast_to(cg, (SSM_STATE, SSM_STATE)).T.astype(BF16)
            rep = _dot(hb[g * GROUP_W:(g + 1) * GROUP_W, :], c_cols)
            for k in range(GROUP_W // LANES):
                ys.append(jnp.sum(jnp.where(eye, rep[k * LANES:(k + 1) * LANES, :], 0.0), axis=0, keepdims=True))
        y_ref[i] = jnp.concatenate(ys, axis=1)


def _sample_ssd(rows, state_all, layer, stacked, nb):
    depth, m, hp, _ = state_all.shape
    state_spec = pl.BlockSpec((None, nb, hp, SSM_STATE), lambda i: (layer, i, 0, 0))
    in_specs = [pl.BlockSpec((nb, 1, _R_END), lambda i: (i, 0, 0)), state_spec]
    args = [rows.reshape(m, 1, _R_END), state_all]
    aliases = {}
    if stacked is not None:
        in_specs.append(pl.BlockSpec(memory_space=pl.ANY))
        args.append(stacked)
        aliases = {2: 0}
    return pl.pallas_call(
        _sssd_kernel,
        grid=(m // nb,),
        in_specs=in_specs,
        out_specs=[state_spec, pl.BlockSpec((nb, 1, hp), lambda i: (i, 0, 0))],
        out_shape=[jax.ShapeDtypeStruct((depth, m, hp, SSM_STATE), F32),
                   jax.ShapeDtypeStruct((m, 1, hp), F32)],
        input_output_aliases=aliases,
        compiler_params=_params(("arbitrary",)),
        name="sample_ssd",
    )(*args)


def _sattn_kernel(q_ref, k_ref, v_ref, o_ref):
    nb = q_ref.shape[0]
    halves = MEM_HEADDIM // LANES
    rows = halves * MEM_HEADS
    row = lax.broadcasted_iota(jnp.int32, (rows, CACHE_ROWS), 0)
    col = lax.broadcasted_iota(jnp.int32, (rows, CACHE_ROWS), 1)
    own = (col & (rows - 1)) == row
    lane = lax.broadcasted_iota(jnp.int32, (1, LANES), 1)
    low_half = (lane & MEM_HEADS) == 0
    for i in range(nb):
        q = q_ref[i]
        q8 = jnp.concatenate([q[:, (h * halves + j) * LANES:(h * halves + j + 1) * LANES]
                              for j in range(halves) for h in range(MEM_HEADS)], axis=0)
        s = _dot_nt(q8.astype(BF16), k_ref[i].astype(BF16))
        t = jnp.sum(jnp.where(own, s, 0.0), axis=0, keepdims=True)
        parts = []
        for b in range(CACHE_ROWS // LANES):
            tb = t[:, b * LANES:(b + 1) * LANES]
            parts.append(tb + jnp.where(low_half, pltpu.roll(tb, LANES - MEM_HEADS, 1),
                                        pltpu.roll(tb, MEM_HEADS, 1)))
        u = jnp.concatenate(parts, axis=1) * (MEM_HEADDIM ** -0.5)
        p = _softmax_rows(jnp.where(own, u, -jnp.inf))
        o8 = _dot(p.astype(BF16), v_ref[i].astype(BF16))
        for j in range(halves):
            for h in range(MEM_HEADS):
                r = j * MEM_HEADS + h
                o_ref[i, :, (h * halves + j) * LANES:(h * halves + j + 1) * LANES] = o8[r:r + 1, :]


def _cache_rows(c):
    depth, m = c.shape[:2]
    c = c.reshape(depth, m, N_MEM, MEM_HEADS, MEM_HEADDIM // LANES, LANES)
    return c.transpose(0, 1, 2, 4, 3, 5).reshape(depth, m, CACHE_ROWS, LANES)


def _sample_attn(q, k_rows, v_rows, layer, nb):
    m = q.shape[0]
    cache = pl.BlockSpec((None, nb, CACHE_ROWS, LANES), lambda i: (layer, i, 0, 0))
    return pl.pallas_call(
        _sattn_kernel,
        grid=(m // nb,),
        in_specs=[pl.BlockSpec((nb, 1, ATTN_W), lambda i: (i, 0, 0)), cache, cache],
        out_specs=pl.BlockSpec((nb, 1, ATTN_W), lambda i: (i, 0, 0)),
        out_shape=jax.ShapeDtypeStruct((m, 1, ATTN_W), F32),
        compiler_params=_params(("arbitrary",)),
        name="sample_attn",
    )(q.reshape(m, 1, ATTN_W), k_rows, v_rows)


def _post_kernel(x_ref, p_ref, acta_ref, xs_ref, ysd_ref, o_ref, vec_ref, wao_ref, wso_ref, wmo_ref, wo_ref,
                 out_ref, *, final_norm):
    y_a = _dot(acta_ref[...].astype(BF16), wao_ref[...])
    gz = ((ysd_ref[...] + _vec(vec_ref, _V_DSK, SSM_INNER) * xs_ref[...])
          * _silu(p_ref[:, _S_SZ:_S_SZ + SSM_INNER]))
    parts = []
    for g in range(SSM_GROUPS):
        gg = gz[:, g * GROUP_W:(g + 1) * GROUP_W]
        parts.append(gg * lax.rsqrt(jnp.mean(gg * gg, axis=-1, keepdims=True) + EPS))
    gn = jnp.concatenate(parts, axis=1) * _vec(vec_ref, _V_SNW, SSM_INNER)
    y_s = _dot(gn.astype(BF16), wso_ref[...])
    mz = p_ref[:, _S_M + ATTN_W:_S_M + 2 * ATTN_W]
    y_m = _dot((_silu(mz) * o_ref[...]).astype(BF16), wmo_ref[...])
    merged = (_sigmoid(p_ref[:, _S_G:_S_G + D_MODEL]) * y_a
              + _sigmoid(p_ref[:, _S_G + D_MODEL:_S_G + 2 * D_MODEL]) * y_s
              + _sigmoid(p_ref[:, _S_G + 2 * D_MODEL:_S_G + 3 * D_MODEL]) * y_m)
    xn = x_ref[...] + _dot(merged.astype(BF16), wo_ref[...])
    out_ref[...] = _rms(xn, _vec(vec_ref, _V_FNW, D_MODEL)) if final_norm else xn


def _sample_post(x, proj, acta, xs, ysd, o, vec, wao, wso, wmo, wo, layer, final_norm):
    m = x.shape[0]
    full = lambda a: _resident(a.shape, (0,) * a.ndim)
    acts = (x, proj, acta, xs, ysd, o)
    return pl.pallas_call(
        functools.partial(_post_kernel, final_norm=final_norm),
        grid=(1,),
        in_specs=([full(a) for a in acts]
                  + [_vec_spec(layer), _layer_mat((CONV_W, D_MODEL), layer), _layer_mat((SSM_INNER, D_MODEL), layer),
                     _layer_mat((ATTN_W, D_MODEL), layer), _layer_mat((D_MODEL, D_MODEL), layer)]),
        out_specs=_resident((m, D_MODEL), (0, 0)),
        out_shape=jax.ShapeDtypeStruct((m, D_MODEL), F32),
        compiler_params=_params(("arbitrary",)),
        name="sample_post",
    )(*acts, vec, wao, wso, wmo, wo)


_TAIL_FIRST_A = _OFF_DT // WBLK
_TAIL_FIRST_B = _OFF_DT // LANES


def _regroup_kernel(a_ref, b_ref, last_ref, o_ref):
    j = pl.program_id(1)

    @pl.when(j < _B_DT)
    def _():
        o_ref[...] = a_ref[...].T.astype(BF16)

    @pl.when((j >= _B_DT) & (j < _B_Q))
    def _():
        heads = WBLK // SSM_HEADDIM
        first = (j - _B_DT) * heads
        rows = jnp.concatenate([jnp.broadcast_to(b_ref[pl.ds(first + k, 1), :], (SSM_HEADDIM, D_MODEL))
                                for k in range(heads)], axis=0)
        o_ref[...] = rows.T.astype(BF16)

    @pl.when(j >= _B_Q)
    def _():
        nxt = jnp.where(j == _N_BLK - 1, last_ref[...], b_ref[...])
        rows = jnp.concatenate([a_ref[SSM_HEADS:, :], nxt[:SSM_HEADS, :]], axis=0)
        o_ref[...] = rows.T.astype(BF16)


def _regroup_w_in(w_in):
    depth, _, n_in = w_in.shape
    w_in = w_in.transpose(0, 2, 1)
    last = jnp.pad(w_in[:, n_in - SSM_HEADS:, :], ((0, 0), (0, LANES - SSM_HEADS), (0, 0)))
    a_idx = lambda j: jnp.where(j < _B_DT, j, jnp.clip(j - _B_Q + _TAIL_FIRST_A, _TAIL_FIRST_A, n_in // WBLK - 1))
    b_idx = lambda j: jnp.where(j < _B_Q, _TAIL_FIRST_B,
                                jnp.minimum(_TAIL_FIRST_B + (WBLK // LANES) * (j - _B_Q + 1), n_in // LANES - 1))
    return pl.pallas_call(
        _regroup_kernel,
        grid=(depth, _N_BLK),
        in_specs=[pl.BlockSpec((None, WBLK, D_MODEL), lambda l, j: (l, a_idx(j), 0)),
                  pl.BlockSpec((None, LANES, D_MODEL), lambda l, j: (l, b_idx(j), 0)),
                  pl.BlockSpec((None, LANES, D_MODEL), lambda l, j: (l, 0, 0))],
        out_specs=pl.BlockSpec((None, D_MODEL, WBLK), lambda l, j: (l, 0, j)),
        out_shape=jax.ShapeDtypeStruct((depth, D_MODEL, _S_END), BF16),
        compiler_params=_params(("arbitrary", "arbitrary")),
        name="regroup_w_in",
    )(w_in, w_in, last)


def _pad_lanes(a):
    return jnp.pad(a, ((0, 0), (0, LANES - a.shape[1])))


def _prompt_tile(seq):
    for t in (512, 256, 128):
        if seq % t == 0:
            return t
    raise ValueError(f"sequence length {seq} must be a multiple of {SSD_CHUNK}")


def kernel(x_prompt, x_sample, mem_prompt, state_conv_a, state_conv_s, state_ssm, cache_mem_k, cache_mem_v,
           norm_w, w_in, conv_a_w, w_a_out, conv_s_w, conv_s_b, dt_bias, a_log, d_skip, ssm_norm_w,
           w_s_out, mem_norm_w, w_mem_kv, w_m_out, w_o, final_norm_w):
    depth = w_in.shape[0]
    bp, seq, _ = x_prompt.shape
    bs = x_sample.shape[0]
    assert x_sample.shape[1] == 1
    tile = _prompt_tile(seq)

    w_all = (_regroup_w_in(w_in),)
    wdt = jnp.pad(w_in[:, :, _OFF_DT:_OFF_M].astype(BF16), ((0, 0), (0, 0), (0, LANES - SSM_HEADS)))
    wao, wso, wmo, wo, wkv = (a.astype(BF16) for a in (w_a_out, w_s_out, w_m_out, w_o, w_mem_kv))
    rep = lambda a: jnp.repeat(a, SSM_HEADDIM, axis=1)
    vec = jnp.concatenate([norm_w, mem_norm_w, conv_s_b, _pad_lanes(dt_bias), _pad_lanes(a_log), rep(d_skip),
                           ssm_norm_w, rep(dt_bias), rep(a_log),
                           jnp.broadcast_to(final_norm_w[None], (depth, D_MODEL))], axis=1)[:, None, :]

    x = x_prompt
    pa, ps, ph = [], [], []
    pk_rows, pv_rows, kt, vb = _mem_kv(mem_prompt, vec, wkv)
    for l in range(depth):
        last = l == depth - 1
        acc, ca = _prompt_am(x, vec, w_all, conv_a_w, wao, wmo, kt, vb, l, tile)
        x, cs, hs = _prompt_ssd(x, acc, vec, w_all, wdt, conv_s_w, wso, wo, l, tile, last)
        pa.append(ca)
        ps.append(cs)
        ph.append(hs.reshape(bp, SSM_HEADS, SSM_HEADDIM, SSM_STATE))
    y_prompt = x

    x = x_sample.reshape(bs, D_MODEL)
    state_all = state_ssm.reshape(depth, bs, SSM_HEADS * SSM_HEADDIM, SSM_STATE)
    k_rows, v_rows = _cache_rows(cache_mem_k), _cache_rows(cache_mem_v)
    scs_all = state_conv_s.transpose(0, 2, 1, 3)
    sa, ss, s_ssm = [], [], None
    for l in range(depth):
        last = l == depth - 1
        proj = _sample_proj(x, vec, w_all, l)
        acta, rows, xs, nca, ncs = _sample_pre(
            proj, state_conv_a[l].reshape(bs, 2 * CONV_W), scs_all, vec, conv_a_w, conv_s_w, l, 32)
        s_ssm, ysd = _sample_ssd(rows, state_all, l, s_ssm, SSD_SEQS_PER_STEP)
        o = _sample_attn(proj[:, _S_M:_S_M + ATTN_W], k_rows, v_rows, l, ATTN_SEQS_PER_STEP)
        x = _sample_post(x, proj, acta, xs, ysd.reshape(bs, SSM_INNER), o.reshape(bs, ATTN_W), vec,
                         wao, wso, wmo, wo, l, last)
        sa.append(nca.reshape(bs, 2, CONV_W))
        ss.append(ncs)
    y_sample = x.reshape(bs, 1, D_MODEL)

    return (y_prompt, y_sample, jnp.stack(pa), jnp.stack(ps), jnp.stack(ph),
            _cache_from_rows(pk_rows), _cache_from_rows(pv_rows),
            jnp.stack(sa), jnp.stack(ss).transpose(0, 2, 1, 3),
            s_ssm.reshape(depth, bs, SSM_HEADS, SSM_HEADDIM, SSM_STATE))
```
